```python
import math
import jax, jax.numpy as jnp
from jax import lax
import numpy as np

D_MODEL = 1024
BATCH = 8
SEQ = 4096
DEPTH = 4

CTX_LEN = 256
GRID_W = 64
N_EVEN = (DEPTH + 1) // 2
N_ODD = DEPTH // 2
EPS = 1e-6

NA_HEADS = 8
NA_DIM = 64
NA_WIN_R = 8
NA_WIN_C = 16
A_W = NA_HEADS * NA_DIM

DN_HEADS = 4
DN_DIM = 128
DN_W = DN_HEADS * DN_DIM
DN_CHUNK = 64
CONV_K = 3
ROPE_BASE = 10000.0

EVEN_IN = 3 * A_W + 4 * DN_W + 4 * DN_HEADS

GM_HALF = 3 * D_MODEL
GM_GROUPS = 8
GM_CHUNK = 128

D_FF = 4 * D_MODEL

kernel_name = 'hybrid_na_gdn_gmlp_diffusion_block'


def _rmsnorm(x, g):
    xf = x.astype(jnp.float32)
    y = xf * lax.rsqrt(jnp.mean(xf * xf, axis=-1, keepdims=True) + EPS)
    return (y * g.astype(jnp.float32)).astype(x.dtype)


def _l2norm(x):
    xf = x.astype(jnp.float32)
    return (xf * lax.rsqrt(jnp.sum(xf * xf, axis=-1, keepdims=True) + EPS)).astype(x.dtype)


def _adaln(cond, w, b):
    m = jax.nn.silu(cond) @ w + b
    return [t[:, None, :] for t in jnp.split(m, 6, axis=-1)]


def _axial_rope(seq_len):
    t = jnp.arange(seq_len)
    row = (t // GRID_W).astype(jnp.float32)
    col = (t % GRID_W).astype(jnp.float32)
    n_freq = DN_DIM // 4
    inv = ROPE_BASE ** (-jnp.arange(n_freq, dtype=jnp.float32) / n_freq)
    ar = row[:, None] * inv
    ac = col[:, None] * inv
    ang = jnp.concatenate([ar, ar, ac, ac], axis=-1)
    return jnp.cos(ang), jnp.sin(ang)


def _apply_rope(x, cos, sin):
    x1, x2, x3, x4 = jnp.split(x, 4, axis=-1)
    rot = jnp.concatenate([-x2, x1, -x4, x3], axis=-1)
    return x * cos[None, :, None, :] + rot * sin[None, :, None, :]


def _short_conv(x, w):
    y = lax.conv_general_dilated(
        x, w[:, None, :].astype(x.dtype), window_strides=(1,),
        padding=[(CONV_K // 2, CONV_K // 2)],
        dimension_numbers=('NWC', 'WIO', 'NWC'), feature_group_count=x.shape[-1])
    return jax.nn.silu(y)


def _neighbourhood_attention(q, k, v, k_ctx, v_ctx, rpb):
    B, L, H, d = q.shape
    rows = L // GRID_W
    wr = min(NA_WIN_R, rows)
    qg = q.reshape(B, rows, GRID_W, H, d)
    kg = k.reshape(B, rows, GRID_W, H, d)
    vg = v.reshape(B, rows, GRID_W, H, d)
    scale = d ** -0.5
    col = jnp.arange(GRID_W)
    cstart = jnp.clip(col - NA_WIN_C // 2, 0, GRID_W - NA_WIN_C)
    col_ok = (col[None, :] >= cstart[:, None]) & (col[None, :] < cstart[:, None] + NA_WIN_C)
    dc_idx = jnp.clip(col[None, :] - col[:, None], -(NA_WIN_C - 1), NA_WIN_C - 1) + NA_WIN_C - 1
    rpb32 = rpb.astype(jnp.float32)

    def one_row(r):
        rs = jnp.clip(r - wr // 2, 0, rows - wr)
        q_r = lax.dynamic_index_in_dim(qg, r, axis=1, keepdims=False)
        k_b = lax.dynamic_slice_in_dim(kg, rs, wr, axis=1)
        v_b = lax.dynamic_slice_in_dim(vg, rs, wr, axis=1)
        dr_idx = rs + jnp.arange(wr) - r + NA_WIN_R - 1
        bias = rpb32[:, dr_idx[None, :, None], dc_idx[:, None, :]]
        bias = jnp.where(col_ok[:, None, :], bias, -jnp.inf)
        s_loc = jnp.einsum('bqhd,bwkhd->bhqwk', q_r, k_b).astype(jnp.float32) * scale + bias[None]
        s_ctx = jnp.einsum('bqhd,bchd->bhqc', q_r, k_ctx).astype(jnp.float32) * scale
        n_loc = wr * GRID_W
        s = jnp.concatenate([s_loc.reshape(B, H, GRID_W, n_loc), s_ctx], axis=-1)
        p = jax.nn.softmax(s, axis=-1).astype(v.dtype)
        o = jnp.einsum('bhqn,bnhd->bqhd', p[..., :n_loc], v_b.reshape(B, n_loc, H, d))
        o = o + jnp.einsum('bhqc,bchd->bqhd', p[..., n_loc:], v_ctx)
        return o

    out = lax.map(one_row, jnp.arange(rows))
    return out.transpose(1, 0, 2, 3, 4).reshape(B, L, H * d)


def _context_attention(q, k, v):
    B, Lc, H, d = q.shape
    s = jnp.einsum('bqhd,bkhd->bhqk', q, k).astype(jnp.float32) * d ** -0.5
    p = jax.nn.softmax(s, axis=-1).astype(v.dtype)
    return jnp.einsum('bhqk,bkhd->bqhd', p, v).reshape(B, Lc, H * d)


def _delta_chunked(q, k, v, beta, logg, s0):
    B, L, H, dk = q.shape
    dv = v.shape[-1]
    n = L // DN_CHUNK
    C = DN_CHUNK

    def blk(t):
        return t.astype(jnp.float32).reshape(B, n, C, H, -1).transpose(1, 0, 3, 2, 4)

    qb, kb, vb = blk(q), blk(k), blk(v)
    bb = beta.astype(jnp.float32).reshape(B, n, C, H).transpose(1, 0, 3, 2)
    gam = jnp.cumsum(logg.astype(jnp.float32).reshape(B, n, C, H).transpose(1, 0, 3, 2), axis=-1)
    idx = jnp.arange(C)
    incl = idx[:, None] >= idx[None, :]
    strict = idx[:, None] > idx[None, :]
    decay = jnp.exp(jnp.where(incl, gam[..., :, None] - gam[..., None, :], -jnp.inf))
    kk = jnp.einsum('nbhrd,nbhjd->nbhrj', kb, kb)
    lmat = jnp.where(strict, bb[..., :, None] * kk * decay, 0.0)
    eg = jnp.exp(gam)
    rhs = jnp.concatenate([bb[..., None] * vb, (bb * eg)[..., None] * kb], axis=-1)
    sol = lax.linalg.triangular_solve(lmat, rhs, left_side=True, lower=True, unit_diagonal=True)
    u_v, w_k = sol[..., :dv], sol[..., dv:]
    aqk = jnp.einsum('nbhrd,nbhjd->nbhrj', qb, kb) * decay
    qg = qb * eg[..., None]
    kdec = kb * jnp.exp(gam[..., -1:] - gam)[..., None]
    gl = jnp.exp(gam[..., -1])

    def step(S, xs):
        u_v_n, w_k_n, aqk_n, qg_n, kdec_n, gl_n = xs
        u = u_v_n - jnp.einsum('bhck,bhkv->bhcv', w_k_n, S)
        o = jnp.einsum('bhck,bhkv->bhcv', qg_n, S) + jnp.einsum('bhcj,bhjv->bhcv', aqk_n, u)
        S = gl_n[..., None, None] * S + jnp.einsum('bhck,bhcv->bhkv', kdec_n, u)
        return S, o

    S, o = lax.scan(step, s0.astype(jnp.float32), (u_v, w_k, aqk, qg, kdec, gl))
    o = o.transpose(1, 0, 3, 2, 4).reshape(B, L, H, dv)
    return o.astype(v.dtype), S


def _dn_inputs(p, conv_w, a_log, dt_bias, cos, sin):
    B, L, _ = p.shape
    qkv = _short_conv(p[..., :3 * DN_W], conv_w)
    q, k, v = [t.reshape(B, L, DN_HEADS, DN_DIM) for t in jnp.split(qkv, 3, axis=-1)]
    q, k = _l2norm(q), _l2norm(k)
    if cos is not None:
        q, k = _apply_rope(q, cos, sin), _apply_rope(k, cos, sin)
    q = q * DN_DIM ** -0.5
    gate = p[..., 3 * DN_W:4 * DN_W]
    off = 4 * DN_W
    beta = jax.nn.sigmoid(p[..., off:off + 2 * DN_HEADS].astype(jnp.float32)).reshape(B, L, 2, DN_HEADS)
    a = p[..., off + 2 * DN_HEADS:off + 4 * DN_HEADS].astype(jnp.float32).reshape(B, L, 2, DN_HEADS)
    logg = -jnp.exp(a_log.astype(jnp.float32)) * jax.nn.softplus(a + dt_bias.astype(jnp.float32))
    return q, k, v, beta, logg, gate


def _gated_deltanet(p_lat, p_ctx, conv_w, a_log, dt_bias, g_out, cos, sin):
    ql, kl, vl, bl, gl, gate_l = _dn_inputs(p_lat, conv_w, a_log, dt_bias, cos, sin)
    qc, kc, vc, bc, gc, gate_c = _dn_inputs(p_ctx, conv_w, a_log, dt_bias, None, None)
    B = p_lat.shape[0]
    zeros = jnp.zeros((B, DN_HEADS, DN_DIM, DN_DIM), jnp.float32)
    flip = lambda t: jnp.flip(t, axis=1)
    oc_f, sc_f = _delta_chunked(qc, kc, vc, bc[:, :, 0], gc[:, :, 0], zeros)
    ol_f, _ = _delta_chunked(ql, kl, vl, bl[:, :, 0], gl[:, :, 0], sc_f)
    oc_b, sc_b = _delta_chunked(flip(qc), flip(kc), flip(vc), flip(bc[:, :, 1]), flip(gc[:, :, 1]), zeros)
    ol_b, _ = _delta_chunked(flip(ql), flip(kl), flip(vl), flip(bl[:, :, 1]), flip(gl[:, :, 1]), sc_b)
    o_l = ol_f + flip(ol_b)
    o_c = oc_f + flip(oc_b)

    def finish(o, gate):
        Bq, Lq = o.shape[0], o.shape[1]
        y = _rmsnorm(o, g_out) * jax.nn.silu(gate.reshape(o.shape))
        return y.reshape(Bq, Lq, DN_W)

    return finish(o_l, gate_l), finish(o_c, gate_c)


def _even_mixer(h_lat, h_ctx, w_in, w_out, rpb, conv_w, a_log, dt_bias, g_out, cos, sin, ctx_out):
    p_lat = h_lat @ w_in
    p_ctx = h_ctx @ w_in
    B, L, _ = h_lat.shape
    Lc = h_ctx.shape[1]
    qa_l, ka_l, va_l = [t.reshape(B, L, NA_HEADS, NA_DIM) for t in jnp.split(p_lat[..., :3 * A_W], 3, axis=-1)]
    qa_c, ka_c, va_c = [t.reshape(B, Lc, NA_HEADS, NA_DIM) for t in jnp.split(p_ctx[..., :3 * A_W], 3, axis=-1)]
    att_l = _neighbourhood_attention(qa_l, ka_l, va_l, ka_c, va_c, rpb)
    dn_l, dn_c = _gated_deltanet(p_lat[..., 3 * A_W:], p_ctx[..., 3 * A_W:], conv_w, a_log, dt_bias, g_out, cos, sin)
    y_lat = jnp.concatenate([att_l, dn_l], axis=-1) @ w_out
    if not ctx_out:
        return y_lat, None
    att_c = _context_attention(qa_c, ka_c, va_c)
    y_ctx = jnp.concatenate([att_c, dn_c], axis=-1) @ w_out
    return y_lat, y_ctx


def _chunk_gmlp(h, w_in, g_v, ws, bs, w_out):
    z = jax.nn.gelu(h @ w_in, approximate=False)
    u, v = z[..., :GM_HALF], z[..., GM_HALF:]
    v = _rmsnorm(v, g_v)
    B, L, E = v.shape
    n = L // GM_CHUNK
    vg = v.reshape(B, n, GM_CHUNK, GM_GROUPS, E // GM_GROUPS)
    mixed = jnp.einsum('gpq,bnqgc->bnpgc', ws, vg) + bs.T[None, None, :, :, None]
    return (u * mixed.reshape(B, L, E)) @ w_out


def _sq_relu_mlp(h, w1, w2):
    a = jax.nn.relu(h @ w1)
    return (a * a) @ w2


def setup_inputs(seed: int = 0) -> dict:
    key = jax.random.key(seed)
    ks = jax.random.split(key, 24)
    f32 = jnp.float32
    D = D_MODEL

    def nrm(k, shape, s):
        return jax.random.normal(k, shape, f32) * s

    x = nrm(ks[0], (BATCH, SEQ, D), 1.0)
    c = nrm(ks[1], (BATCH, D), 1.0)
    ctx = nrm(ks[2], (BATCH, CTX_LEN, D), 1.0)
    c_ctx = nrm(ks[3], (D,), 1.0)
    w_ada = nrm(ks[4], (DEPTH, D, 6 * D), 0.5 * D ** -0.5)
    b_ada = nrm(ks[5], (DEPTH, 6 * D), 0.02)
    g_norm_mix = 1.0 + nrm(ks[6], (DEPTH, D), 0.02)
    g_norm_ffn = 1.0 + nrm(ks[7], (DEPTH, D), 0.02)
    w_in_even = nrm(ks[8], (N_EVEN, D, EVEN_IN), D ** -0.5)
    w_out_even = nrm(ks[9], (N_EVEN, A_W + DN_W, D), (A_W + DN_W) ** -0.5)
    na_rpb = nrm(ks[10], (N_EVEN, NA_HEADS, 2 * NA_WIN_R - 1, 2 * NA_WIN_C - 1), 0.1)
    dn_conv = nrm(ks[11], (N_EVEN, CONV_K, 3 * DN_W), CONV_K ** -0.5)
    dn_a_log = jnp.log(jax.random.uniform(ks[12], (N_EVEN, 2, DN_HEADS), f32, 1.0, 16.0))
    dt = jnp.exp(jax.random.uniform(ks[13], (N_EVEN, 2, DN_HEADS), f32, math.log(1e-3), math.log(1e-1)))
    dn_dt_bias = dt + jnp.log(-jnp.expm1(-dt))
    dn_g_out = 1.0 + nrm(ks[14], (N_EVEN, DN_DIM), 0.02)
    w_in_odd = nrm(ks[15], (N_ODD, D, 2 * GM_HALF), D ** -0.5)
    gm_g_v = 1.0 + nrm(ks[16], (N_ODD, GM_HALF), 0.02)
    gm_ws = nrm(ks[17], (N_ODD, GM_GROUPS, GM_CHUNK, GM_CHUNK), GM_CHUNK ** -0.5)
    gm_bs = 1.0 + nrm(ks[18], (N_ODD, GM_GROUPS, GM_CHUNK), 0.1)
    w_out_odd = nrm(ks[19], (N_ODD, GM_HALF, D), GM_HALF ** -0.5)
    w_ff1 = nrm(ks[20], (DEPTH, D, D_FF), D ** -0.5)
    w_ff2 = nrm(ks[21], (DEPTH, D_FF, D), D_FF ** -0.5)
    g_final = 1.0 + nrm(ks[22], (D,), 0.02)
    return {'x': x, 'c': c, 'ctx': ctx, 'c_ctx': c_ctx, 'w_ada': w_ada, 'b_ada': b_ada,
            'g_norm_mix': g_norm_mix, 'g_norm_ffn': g_norm_ffn, 'w_in_even': w_in_even,
            'w_out_even': w_out_even, 'na_rpb': na_rpb, 'dn_conv': dn_conv, 'dn_a_log': dn_a_log,
            'dn_dt_bias': dn_dt_bias, 'dn_g_out': dn_g_out, 'w_in_odd': w_in_odd, 'gm_g_v': gm_g_v,
            'gm_ws': gm_ws, 'gm_bs': gm_bs, 'w_out_odd': w_out_odd, 'w_ff1': w_ff1, 'w_ff2': w_ff2,
            'g_final': g_final}


def reference(x, c, ctx, c_ctx, w_ada, b_ada, g_norm_mix, g_norm_ffn, w_in_even, w_out_even, na_rpb,
              dn_conv, dn_a_log, dn_dt_bias, dn_g_out, w_in_odd, gm_g_v, gm_ws, gm_bs, w_out_odd,
              w_ff1, w_ff2, g_final):
    seq_len = x.shape[1]
    cos, sin = _axial_rope(seq_len)
    x_lat, x_ctx = x, ctx
    cc = c_ctx[None, :]
    for l in range(DEPTH):
        ctx_live = any(j % 2 == 0 for j in range(l + 1, DEPTH))
        sh1, sc1, gt1, sh2, sc2, gt2 = _adaln(c, w_ada[l], b_ada[l])
        h_lat = _rmsnorm(x_lat, g_norm_mix[l]) * (1 + sc1) + sh1
        if l % 2 == 0 or ctx_live:
            csh1, csc1, cgt1, csh2, csc2, cgt2 = _adaln(cc, w_ada[l], b_ada[l])
            h_ctx = _rmsnorm(x_ctx, g_norm_mix[l]) * (1 + csc1) + csh1
        if l % 2 == 0:
            e = l // 2
            y_lat, y_ctx = _even_mixer(h_lat, h_ctx, w_in_even[e], w_out_even[e], na_rpb[e], dn_conv[e],
                                       dn_a_log[e], dn_dt_bias[e], dn_g_out[e], cos, sin, ctx_live)
        else:
            o = l // 2
            y_lat = _chunk_gmlp(h_lat, w_in_odd[o], gm_g_v[o], gm_ws[o], gm_bs[o], w_out_odd[o])
            if ctx_live:
                y_ctx = _chunk_gmlp(h_ctx, w_in_odd[o], gm_g_v[o], gm_ws[o], gm_bs[o], w_out_odd[o])
        x_lat = x_lat + gt1 * y_lat
        x_lat = x_lat + gt2 * _sq_relu_mlp(_rmsnorm(x_lat, g_norm_ffn[l]) * (1 + sc2) + sh2, w_ff1[l], w_ff2[l])
        if ctx_live:
            x_ctx = x_ctx + cgt1 * y_ctx
            x_ctx = x_ctx + cgt2 * _sq_relu_mlp(_rmsnorm(x_ctx, g_norm_ffn[l]) * (1 + csc2) + csh2, w_ff1[l], w_ff2[l])
    return _rmsnorm(x_lat, g_final)
```

```python
import functools
import math

import jax
import jax.numpy as jnp
from jax import lax
from jax.experimental import pallas as pl
from jax.experimental.pallas import tpu as pltpu

EPS = 1e-6
GRID_W = 64
NA_HEADS, NA_DIM, NA_WIN_R, NA_WIN_C = 8, 64, 8, 16
A_W = NA_HEADS * NA_DIM
DN_HEADS, DN_DIM, DN_CHUNK = 4, 128, 64
DN_W = DN_HEADS * DN_DIM
ROPE_BASE = 10000.0
GM_GROUPS, GM_CHUNK = 8, 128

LANES = 128
TM = 256
VMEM_LIMIT = 56 * 1024 * 1024
F32, BF16 = jnp.float32, jnp.bfloat16


def _cparams(n_axes):
    return pltpu.CompilerParams(dimension_semantics=("parallel",) * n_axes, vmem_limit_bytes=VMEM_LIMIT)


def _tok_spec(cols, off=0, col_block=0):
    return pl.BlockSpec((None, TM, cols), lambda b, j: (b, j + off, col_block))


def _mod_spec(d, ctx_tile):
    if ctx_tile:
        return pl.BlockSpec((None, 1, d), lambda b, j: (jnp.where(j == 0, 8, b), 0, 0))
    return pl.BlockSpec((None, 1, d), lambda b, j: (b, 0, 0))


def _full_spec(shape):
    n = len(shape)
    return pl.BlockSpec(shape, lambda b, j: (0,) * n)


def _norm_mod(x, g, sc, sh):
    y = x * lax.rsqrt(jnp.mean(x * x, axis=-1, keepdims=True) + EPS)
    return (y * g) * (1.0 + sc) + sh


def _silu(x):
    return x * jax.nn.sigmoid(x)


def _adaln_kernel(cond_ref, w_ref, b_ref, o_ref):
    a = _silu(cond_ref[...]).astype(BF16)
    o_ref[...] = jnp.dot(a, w_ref[...].astype(BF16), preferred_element_type=F32) + b_ref[...]


def _adaln(cond, w_ada, b_ada):
    depth, d, n = w_ada.shape
    tn = n // 4
    return pl.pallas_call(
        _adaln_kernel,
        grid=(depth, n // tn),
        in_specs=[pl.BlockSpec((16, d), lambda l, j: (0, 0)),
                  pl.BlockSpec((None, d, tn), lambda l, j: (l, 0, j)),
                  pl.BlockSpec((None, 1, tn), lambda l, j: (l, 0, j))],
        out_specs=pl.BlockSpec((None, 16, tn), lambda l, j: (l, 0, j)),
        out_shape=jax.ShapeDtypeStruct((depth, 16, n), F32),
        compiler_params=_cparams(2), name="adaln",
    )(cond, w_ada, b_ada.reshape(depth, 1, n))


def _even_in_kernel(x_ref, g_ref, sc_ref, sh_ref, w_ref, qa_ref, qb_ref, gate_ref, sm_ref):
    h = _norm_mod(x_ref[...], g_ref[...], sc_ref[...], sh_ref[...]).astype(BF16)
    p = jnp.dot(h, w_ref[...], preferred_element_type=F32)
    qa_ref[...] = p[:, :3 * A_W].astype(BF16)
    qb_ref[...] = p[:, 3 * A_W:3 * A_W + 3 * DN_W].astype(BF16)
    gate_ref[...] = p[:, 3 * A_W + 3 * DN_W:3 * A_W + 4 * DN_W].astype(BF16)
    sm_ref[...] = p[:, 3 * A_W + 4 * DN_W:]


def _even_in(x, g, sc, sh, w_pad):
    b, t, d = x.shape
    n = w_pad.shape[1]
    outs = [(3 * A_W, BF16), (3 * DN_W, BF16), (DN_W, BF16), (LANES, F32)]
    return pl.pallas_call(
        _even_in_kernel,
        grid=(b, t // TM),
        in_specs=[_tok_spec(d), _full_spec((1, d)), _mod_spec(d, True), _mod_spec(d, True),
                  _full_spec((d, n))],
        out_specs=[_tok_spec(c) for c, _ in outs],
        out_shape=[jax.ShapeDtypeStruct((b, t, c), dt) for c, dt in outs],
        compiler_params=_cparams(2), name="even_in",
    )(x, g, sc, sh, w_pad)


def _softmax_pv(q, keys, vals, biases):
    lane = lax.broadcasted_iota(jnp.int32, q.shape, 1)
    out = jnp.zeros(q.shape, F32)
    for hh in range(2):
        mine = (lane // NA_DIM) == hh
        qm = jnp.where(mine, q, jnp.zeros_like(q))
        ss = []
        for kb, bias in zip(keys, biases):
            s = lax.dot_general(qm, kb, (((1,), (1,)), ((), ())), preferred_element_type=F32)
            ss.append(s if bias is None else s + bias(hh))
        m = functools.reduce(jnp.maximum, [jnp.max(s, axis=-1, keepdims=True) for s in ss])
        ps = [jnp.exp(s - m) for s in ss]
        den = functools.reduce(jnp.add, [jnp.sum(p, axis=-1, keepdims=True) for p in ps])
        o = functools.reduce(jnp.add, [jnp.dot(p.astype(BF16), vb, preferred_element_type=F32)
                                       for p, vb in zip(ps, vals)])
        out = jnp.where(mine, o / den, out)
    return out


def _na_kernel(q_ref, k_ref, v_ref, bias_ref, o_ref, *, lc, rows, ctx_steps):
    step = pl.program_id(1)
    scale = NA_DIM ** -0.5
    n_loc = NA_WIN_R * GRID_W

    def latent_row():
        r = step - ctx_steps
        rs = jnp.clip(r - NA_WIN_R // 2, 0, rows - NA_WIN_R)
        start = pl.multiple_of(lc + rs * GRID_W, GRID_W)
        for pair in range(NA_HEADS // 2):
            cs = slice(pair * LANES, (pair + 1) * LANES)
            q = q_ref[:, cs] * scale
            keys = [k_ref[pl.ds(start, n_loc), cs], k_ref[0:lc, cs]]
            vals = [v_ref[pl.ds(start, n_loc), cs], v_ref[0:lc, cs]]
            o = _softmax_pv(q, keys, vals, [lambda hh, pair=pair: bias_ref[2 * pair + hh], None])
            o_ref[:, cs] = o.astype(o_ref.dtype)

    def context_block():
        for pair in range(NA_HEADS // 2):
            cs = slice(pair * LANES, (pair + 1) * LANES)
            q = q_ref[:, cs] * scale
            o = _softmax_pv(q, [k_ref[0:lc, cs]], [v_ref[0:lc, cs]], [None])
            o_ref[:, cs] = o.astype(o_ref.dtype)

    if ctx_steps:
        pl.when(step < ctx_steps)(context_block)
        pl.when(step >= ctx_steps)(latent_row)
    else:
        latent_row()


def _na_bias_table(rpb):
    col = jnp.arange(GRID_W)
    cstart = jnp.clip(col - NA_WIN_C // 2, 0, GRID_W - NA_WIN_C)
    col_ok = (col[None, :] >= cstart[:, None]) & (col[None, :] < cstart[:, None] + NA_WIN_C)
    dc_idx = jnp.clip(col[None, :] - col[:, None], -(NA_WIN_C - 1), NA_WIN_C - 1) + NA_WIN_C - 1
    rpb32 = rpb.astype(F32)
    tabs = []
    for v in range(NA_WIN_R):
        dr_idx = jnp.arange(NA_WIN_R) - v + NA_WIN_R - 1
        bias = rpb32[:, dr_idx[None, :, None], dc_idx[:, None, :]]
        bias = jnp.where(col_ok[:, None, :], bias, -jnp.inf)
        tabs.append(bias.reshape(NA_HEADS, GRID_W, NA_WIN_R * GRID_W))
    return jnp.stack(tabs)


def _na_attention(qkv_a, bias_tab, lc, ctx_out):
    b, t, _ = qkv_a.shape
    rows = (t - lc) // GRID_W
    assert rows >= NA_WIN_R and lc % GRID_W == 0
    ctx_steps = lc // GRID_W if ctx_out else 0
    q_off = 0 if ctx_out else lc // GRID_W

    def bias_map(bi, s):
        r = jnp.maximum(s - ctx_steps, 0)
        return (r - jnp.clip(r - NA_WIN_R // 2, 0, rows - NA_WIN_R), 0, 0, 0)

    n_out = t if ctx_out else t - lc
    return pl.pallas_call(
        functools.partial(_na_kernel, lc=lc, rows=rows, ctx_steps=ctx_steps),
        grid=(b, ctx_steps + rows),
        in_specs=[pl.BlockSpec((None, GRID_W, A_W), lambda bi, s: (bi, s + q_off, 0)),
                  pl.BlockSpec((None, t, A_W), lambda bi, s: (bi, 0, 1)),
                  pl.BlockSpec((None, t, A_W), lambda bi, s: (bi, 0, 2)),
                  pl.BlockSpec((None, NA_HEADS, GRID_W, NA_WIN_R * GRID_W), bias_map)],
        out_specs=pl.BlockSpec((None, GRID_W, A_W), lambda bi, s: (bi, s, 0)),
        out_shape=jax.ShapeDtypeStruct((b, n_out, A_W), BF16),
        compiler_params=_cparams(2), name="na_attention",
    )(qkv_a, qkv_a, qkv_a, bias_tab)


def _chunk_scan(x, reverse):
    n = x.shape[0]
    pos = lax.broadcasted_iota(jnp.int32, x.shape, 0) % DN_CHUNK
    s = 1
    while s < DN_CHUNK:
        if reverse:
            x = x + jnp.where(pos < DN_CHUNK - s, pltpu.roll(x, n - s, 0), 0.0)
        else:
            x = x + jnp.where(pos >= s, pltpu.roll(x, s, 0), 0.0)
        s *= 2
    return x


def _dn_prep_kernel(x_ref, prev_ref, next_ref, sm_ref, cw_ref, cos_ref, sina_ref, sinb_ref, alog_ref, dtb_ref,
                    q_ref, k_ref, v_ref, col_ref, row_ref, *, n_tiles):
    j = pl.program_id(1)
    x = x_ref[...].astype(F32)
    row = lax.broadcasted_iota(jnp.int32, x.shape, 0)
    has_prev = jnp.logical_and(j != 0, j != 1)
    has_next = jnp.logical_and(j != 0, j != n_tiles - 1)
    halo_p = jnp.where(has_prev, prev_ref[15:16, :].astype(F32), 0.0)
    halo_n = jnp.where(has_next, next_ref[0:1, :].astype(F32), 0.0)
    xm = jnp.where(row == 0, halo_p, pltpu.roll(x, 1, 0))
    xp = jnp.where(row == TM - 1, halo_n, pltpu.roll(x, TM - 1, 0))
    y = _silu(cw_ref[0:1, :] * xm + cw_ref[1:2, :] * x + cw_ref[2:3, :] * xp)

    cos, sina, sinb = cos_ref[...], sina_ref[...], sinb_ref[...]
    for h in range(DN_HEADS):
        for base, o_ref, scale in ((0, q_ref, DN_DIM ** -0.5), (DN_W, k_ref, None)):
            cs = slice(base + h * DN_DIM, base + (h + 1) * DN_DIM)
            z = y[:, cs]
            z = z * lax.rsqrt(jnp.sum(z * z, axis=-1, keepdims=True) + EPS)
            z = z * cos + pltpu.roll(z, 3 * DN_DIM // 4, 1) * sina + pltpu.roll(z, DN_DIM // 4, 1) * sinb
            if scale is not None:
                z = z * scale
            o_ref[:, h * DN_DIM:(h + 1) * DN_DIM] = z.astype(o_ref.dtype)
    v_ref[...] = y[:, 2 * DN_W:].astype(v_ref.dtype)

    sm = sm_ref[...]
    lane = lax.broadcasted_iota(jnp.int32, sm.shape, 1)
    logg = -jnp.exp(alog_ref[...]) * jax.nn.softplus(sm + dtb_ref[...])
    gam = jnp.where(lane < 3 * DN_HEADS, _chunk_scan(logg, False), _chunk_scan(logg, True))
    col = jnp.where(lane < 2 * DN_HEADS, jax.nn.sigmoid(sm), gam)
    col_ref[...] = col
    row_ref[...] = col.T[0:4 * DN_HEADS, :]


def _rope_tables(lc, seq):
    t = jnp.arange(seq)
    rowp = (t // GRID_W).astype(F32)
    colp = (t % GRID_W).astype(F32)
    n_freq = DN_DIM // 4
    inv = ROPE_BASE ** (-jnp.arange(n_freq, dtype=F32) / n_freq)
    ar, ac = rowp[:, None] * inv, colp[:, None] * inv
    ang = jnp.concatenate([ar, ar, ac, ac], axis=-1)
    cos, sin = jnp.cos(ang), jnp.sin(ang)
    quarter = (jnp.arange(DN_DIM) // n_freq) % 2
    sina = jnp.where(quarter == 0, -sin, 0.0)
    sinb = jnp.where(quarter == 1, sin, 0.0)
    pad = lambda a, v: jnp.concatenate([jnp.full((lc, DN_DIM), v, F32), a], axis=0)
    return pad(cos, 1.0), pad(sina, 0.0), pad(sinb, 0.0)


def _dn_prep(qkv_b, small, conv_w, a_log, dt_bias, rope):
    b, t, _ = qkv_b.shape
    n_tiles = t // TM
    hb = TM // 16
    pad16 = lambda a: jnp.zeros((1, LANES), F32).at[0, 2 * DN_HEADS:4 * DN_HEADS].set(a.reshape(-1).astype(F32))
    outs = [(DN_W, BF16), (DN_W, BF16), (DN_W, BF16), (LANES, F32)]
    res = pl.pallas_call(
        functools.partial(_dn_prep_kernel, n_tiles=n_tiles),
        grid=(b, n_tiles),
        in_specs=[_tok_spec(3 * DN_W),
                  pl.BlockSpec((None, 16, 3 * DN_W), lambda bi, j: (bi, jnp.maximum(j * hb - 1, 0), 0)),
                  pl.BlockSpec((None, 16, 3 * DN_W), lambda bi, j: (bi, jnp.minimum((j + 1) * hb, t // 16 - 1), 0)),
                  _tok_spec(LANES),
                  _full_spec((3, 3 * DN_W)),
                  pl.BlockSpec((TM, DN_DIM), lambda bi, j: (j, 0)),
                  pl.BlockSpec((TM, DN_DIM), lambda bi, j: (j, 0)),
                  pl.BlockSpec((TM, DN_DIM), lambda bi, j: (j, 0)),
                  _full_spec((1, LANES)), _full_spec((1, LANES))],
        out_specs=[_tok_spec(c) for c, _ in outs] + [pl.BlockSpec((None, 4 * DN_HEADS, TM), lambda bi, j: (bi, 0, j))],
        out_shape=[jax.ShapeDtypeStruct((b, t, c), dt) for c, dt in outs]
        + [jax.ShapeDtypeStruct((b, 4 * DN_HEADS, t), F32)],
        compiler_params=_cparams(2), name="dn_prep",
    )(qkv_b, qkv_b, qkv_b, small, conv_w.astype(F32), *rope, pad16(a_log), pad16(dt_bias))
    q, k, v, colsm, rowsm = res
    n_chunks = t // DN_CHUNK
    rowsm = rowsm.reshape(b, 4 * DN_HEADS, n_chunks, DN_CHUNK).transpose(2, 0, 1, 3)
    return q, k, v, colsm, rowsm


def _dn_chain_step(d, h, q_all, k_all, v_all, cm, rm, s_prev):
    c = DN_CHUNK
    cs = slice(h * DN_DIM, (h + 1) * DN_DIM)
    q, k, v = q_all[:, cs], k_all[:, cs], v_all[:, cs]
    ib, ig = d * DN_HEADS + h, 2 * DN_HEADS + d * DN_HEADS + h
    beta = cm[:, ib:ib + 1]
    gam_c = cm[:, ig:ig + 1]
    gam_r = rm[ig:ig + 1, :]
    ri = lax.broadcasted_iota(jnp.int32, (c, c), 0)
    ci = lax.broadcasted_iota(jnp.int32, (c, c), 1)
    if d == 0:
        incl, strict, g_last = ri >= ci, ri > ci, gam_c[c - 1:c, :]
    else:
        incl, strict, g_last = ri <= ci, ri < ci, gam_c[0:1, :]
    decay = jnp.exp(jnp.where(incl, gam_c - gam_r, -jnp.inf))
    kq = jnp.concatenate([k, q], axis=0)
    g = lax.dot_general(kq, k, (((1,), (1,)), ((), ())), preferred_element_type=F32)
    kk, qk = g[:c], g[c:]
    lneg = jnp.where(strict, -(beta * kk * decay), 0.0)
    aqk = qk * decay
    eg = jnp.exp(gam_c)
    kf, vf, qf = k.astype(F32), v.astype(F32), q.astype(F32)
    rhs = jnp.concatenate([(beta * eg) * kf, beta * vf], axis=1).astype(BF16)
    eye = (ri == ci).astype(F32)
    same_blk = lambda s: (ri // s) == (ci // s)
    dneg = jnp.where(same_blk(4), lneg, 0.0)
    dnb = dneg.astype(BF16)
    dsq = jnp.dot(dnb, dnb, preferred_element_type=F32)
    t = (eye + dneg) + jnp.dot((eye + dneg).astype(BF16), dsq.astype(BF16), preferred_element_type=F32)
    s = 4
    while s < c:
        eneg = jnp.where(jnp.logical_and(same_blk(2 * s), jnp.logical_not(same_blk(s))), lneg, 0.0)
        tb = t.astype(BF16)
        te = jnp.dot(tb, eneg.astype(BF16), preferred_element_type=F32)
        t = t + jnp.dot(te.astype(BF16), tb, preferred_element_type=F32)
        s *= 2
    sol = jnp.dot(t.astype(BF16), rhs, preferred_element_type=F32).astype(BF16)
    kdec = (kf * jnp.exp(g_last - gam_c)).astype(BF16)
    ab = lax.dot_general(kdec, sol, (((0,), (0,)), ((), ())), preferred_element_type=F32)
    qo = jnp.dot(aqk.astype(BF16), sol, preferred_element_type=F32)
    qeff = (qf * eg - qo[:, :DN_DIM]).astype(BF16)
    sb = s_prev.astype(BF16)
    o = jnp.dot(qeff, sb, preferred_element_type=F32) + qo[:, DN_DIM:]
    s_new = (jnp.exp(g_last) * s_prev - jnp.dot(ab[:, :DN_DIM].astype(BF16), sb, preferred_element_type=F32)
             + ab[:, DN_DIM:])
    return o, s_new


def _dn_chain_kernel(qf_ref, kf_ref, vf_ref, cf_ref, rf_ref, qb_ref, kb_ref, vb_ref, cb_ref, rb_ref,
                     of_ref, ob_ref, s_ref, *, nb):
    @pl.when(pl.program_id(0) == 0)
    def _():
        s_ref[...] = jnp.zeros_like(s_ref)

    def body(b, carry):
        for d, (q_r, k_r, v_r, c_r, r_r, o_r) in enumerate(((qf_ref, kf_ref, vf_ref, cf_ref, rf_ref, of_ref),
                                                           (qb_ref, kb_ref, vb_ref, cb_ref, rb_ref, ob_ref))):
            q_all, k_all, v_all, cm, rm = q_r[b], k_r[b], v_r[b], c_r[b], r_r[b]
            for h in range(DN_HEADS):
                o, s_new = _dn_chain_step(d, h, q_all, k_all, v_all, cm, rm, s_ref[d, b, h])
                s_ref[d, b, h] = s_new
                o_r[b, :, h * DN_DIM:(h + 1) * DN_DIM] = o
        return carry

    lax.fori_loop(0, nb, body, 0)


def _dn_chain(q, k, v, colsm, rowsm, lc):
    b, t, _ = q.shape
    n_chunks = t // DN_CHUNK
    nc_ctx = lc // DN_CHUNK
    fwd = lambda i: i
    bwd = lambda i: jnp.where(i < nc_ctx, nc_ctx - 1 - i, n_chunks - 1 + nc_ctx - i)
    tok = lambda cols, cm: pl.BlockSpec((b, DN_CHUNK, cols), lambda i: (0, cm(i), 0))
    rowspec = lambda cm: pl.BlockSpec((None, b, 4 * DN_HEADS, DN_CHUNK), lambda i: (cm(i), 0, 0, 0))
    in_specs = []
    for cm in (fwd, bwd):
        in_specs += [tok(DN_W, cm), tok(DN_W, cm), tok(DN_W, cm), tok(LANES, cm), rowspec(cm)]
    return pl.pallas_call(
        functools.partial(_dn_chain_kernel, nb=b),
        grid=(n_chunks,),
        in_specs=in_specs,
        out_specs=[tok(DN_W, fwd), tok(DN_W, bwd)],
        out_shape=[jax.ShapeDtypeStruct((b, t, DN_W), F32)] * 2,
        scratch_shapes=[pltpu.VMEM((2, b, DN_HEADS, DN_DIM, DN_DIM), F32)],
        compiler_params=pltpu.CompilerParams(dimension_semantics=("arbitrary",), vmem_limit_bytes=VMEM_LIMIT),
        name="dn_chain",
    )(q, k, v, colsm, rowsm, q, k, v, colsm, rowsm)


def _even_out_kernel(x_ref, att_ref, of_ref, ob_ref, gate_ref, gout_ref, w_ref, gt_ref, o_ref):
    o = of_ref[...] + ob_ref[...]
    gate = gate_ref[...].astype(F32)
    parts = [att_ref[...]]
    for h in range(DN_HEADS):
        cs = slice(h * DN_DIM, (h + 1) * DN_DIM)
        oh = o[:, cs]
        yh = oh * lax.rsqrt(jnp.mean(oh * oh, axis=-1, keepdims=True) + EPS) * gout_ref[...]
        parts.append((yh * _silu(gate[:, cs])).astype(BF16))
    a = jnp.concatenate(parts, axis=1)
    y = jnp.dot(a, w_ref[...], preferred_element_type=F32)
    o_ref[...] = x_ref[...] + gt_ref[...] * y


def _even_out(x, att, o_f, o_b, gate, g_out, w_out, gt, ctx_out):
    b, t, d = x.shape
    off = 0 if ctx_out else 1
    n_t = t // TM - off
    return pl.pallas_call(
        _even_out_kernel,
        grid=(b, n_t),
        in_specs=[_tok_spec(d, off), _tok_spec(A_W), _tok_spec(DN_W, off), _tok_spec(DN_W, off), _tok_spec(DN_W, off),
                  _full_spec((1, DN_DIM)), _full_spec((A_W + DN_W, d)), _mod_spec(d, ctx_out)],
        out_specs=_tok_spec(d),
        out_shape=jax.ShapeDtypeStruct((b, n_t * TM, d), F32),
        compiler_params=_cparams(2), name="even_out",
    )(x, att, o_f, o_b, gate, g_out, w_out, gt)


def _gelu(x):
    return 0.5 * x * (1.0 + lax.erf(x * (2.0 ** -0.5)))


def _gmlp_kernel(x_ref, g_ref, sc_ref, sh_ref, wi_ref, gv_ref, ws_ref, bs_ref, wo_ref, gt_ref, o_ref, t_ref):
    x = x_ref[...]
    h = _norm_mod(x, g_ref[...], sc_ref[...], sh_ref[...]).astype(BF16)
    half = wi_ref.shape[1] // 2
    gw = half // GM_GROUPS
    v = _gelu(jnp.dot(h, wi_ref[:, half:], preferred_element_type=F32))
    v = v * lax.rsqrt(jnp.mean(v * v, axis=-1, keepdims=True) + EPS) * gv_ref[...]
    vb = v.astype(BF16)
    for g in range(GM_GROUPS):
        cs = slice(g * gw, (g + 1) * gw)
        u = _gelu(jnp.dot(h, wi_ref[:, cs], preferred_element_type=F32))
        bias = jnp.concatenate([bs_ref[g]] * (gw // LANES), axis=1)
        for n in range(TM // GM_CHUNK):
            rs = slice(n * GM_CHUNK, (n + 1) * GM_CHUNK)
            mixed = jnp.dot(ws_ref[g], vb[rs, cs], preferred_element_type=F32) + bias
            t_ref[rs, cs] = (u[rs] * mixed).astype(BF16)
    y = jnp.dot(t_ref[...], wo_ref[...], preferred_element_type=F32)
    o_ref[...] = x + gt_ref[...] * y


def _gmlp(x, g, sc, sh, w_in, g_v, ws, bs, w_out, gt, has_ctx):
    b, t, d = x.shape
    half = w_in.shape[1] // 2
    assert (half // GM_GROUPS) % LANES == 0
    return pl.pallas_call(
        _gmlp_kernel,
        grid=(b, t // TM),
        in_specs=[_tok_spec(d), _full_spec((1, d)), _mod_spec(d, has_ctx), _mod_spec(d, has_ctx),
                  _full_spec(w_in.shape), _full_spec((1, half)), _full_spec(ws.shape), _full_spec(bs.shape),
                  _full_spec(w_out.shape), _mod_spec(d, has_ctx)],
        out_specs=_tok_spec(d),
        out_shape=jax.ShapeDtypeStruct((b, t, d), F32),
        scratch_shapes=[pltpu.VMEM((TM, half), BF16)],
        compiler_params=_cparams(2), name="gmlp",
    )(x, g, sc, sh, w_in, g_v, ws, bs, w_out, gt)


def _ffn_kernel(x_ref, g_ref, sc_ref, sh_ref, w1_ref, w2_ref, gt_ref, gf_ref, o_ref, *, final_norm):
    x = x_ref[...]
    h = _norm_mod(x, g_ref[...], sc_ref[...], sh_ref[...]).astype(BF16)
    a = jnp.maximum(jnp.dot(h, w1_ref[...], preferred_element_type=F32), 0.0)
    y = jnp.dot((a * a).astype(BF16), w2_ref[...], preferred_element_type=F32)
    out = x + gt_ref[...] * y
    if final_norm:
        out = out * lax.rsqrt(jnp.mean(out * out, axis=-1, keepdims=True) + EPS) * gf_ref[...]
    o_ref[...] = out


def _ffn(x, g, sc, sh, w1, w2, gt, g_final, has_ctx, final_norm):
    b, t, d = x.shape
    return pl.pallas_call(
        functools.partial(_ffn_kernel, final_norm=final_norm),
        grid=(b, t // TM),
        in_specs=[_tok_spec(d), _full_spec((1, d)), _mod_spec(d, has_ctx), _mod_spec(d, has_ctx),
                  _full_spec(w1.shape), _full_spec(w2.shape), _mod_spec(d, has_ctx), _full_spec((1, d))],
        out_specs=_tok_spec(d),
        out_shape=jax.ShapeDtypeStruct((b, t, d), F32),
        compiler_params=_cparams(2), name="ffn",
    )(x, g, sc, sh, w1, w2, gt, g_final)


def kernel(x, c, ctx, c_ctx, w_ada, b_ada, g_norm_mix, g_norm_ffn, w_in_even, w_out_even, na_rpb, dn_conv, dn_a_log,
           dn_dt_bias, dn_g_out, w_in_odd, gm_g_v, gm_ws, gm_bs, w_out_odd, w_ff1, w_ff2, g_final):
    nb, seq, d = x.shape
    lc = ctx.shape[1]
    depth = w_ada.shape[0]
    assert nb <= 8 and seq % TM == 0 and lc == TM and seq % GRID_W == 0

    cond = jnp.zeros((16, d), F32).at[:nb].set(c).at[8].set(c_ctx)
    mods = _adaln(cond, w_ada, b_ada).reshape(depth, 16, 6, 1, d)
    rope = _rope_tables(lc, seq)
    row = lambda a: a.reshape(1, -1).astype(F32)

    xs = jnp.concatenate([ctx, x], axis=1)
    has_ctx = True
    for l in range(depth):
        ctx_live = any(j % 2 == 0 for j in range(l + 1, depth))
        sh1, sc1, gt1, sh2, sc2, gt2 = [mods[l, :, i] for i in range(6)]
        if l % 2 == 0:
            assert has_ctx
            e = l // 2
            w_pad = jnp.pad(w_in_even[e], ((0, 0), (0, LANES - 4 * DN_HEADS))).astype(BF16)
            qkv_a, qkv_b, gate, small = _even_in(xs, row(g_norm_mix[l]), sc1, sh1, w_pad)
            att = _na_attention(qkv_a, _na_bias_table(na_rpb[e]), lc, ctx_live)
            q, k, v, colsm, rowsm = _dn_prep(qkv_b, small, dn_conv[e], dn_a_log[e], dn_dt_bias[e], rope)
            o_f, o_b = _dn_chain(q, k, v, colsm, rowsm, lc)
            xs = _even_out(xs, att, o_f, o_b, gate, row(dn_g_out[e]), w_out_even[e].astype(BF16), gt1, ctx_live)
            has_ctx = ctx_live
        else:
            o = l // 2
            bs = jnp.broadcast_to(gm_bs[o].astype(F32)[:, :, None], (GM_GROUPS, GM_CHUNK, LANES))
            xs = _gmlp(xs, row(g_norm_mix[l]), sc1, sh1, w_in_odd[o].astype(BF16), row(gm_g_v[o]),
                       gm_ws[o].astype(BF16), bs, w_out_odd[o].astype(BF16), gt1, has_ctx)
        xs = _ffn(xs, row(g_norm_ffn[l]), sc2, sh2, w_ff1[l].astype(BF16), w_ff2[l].astype(BF16), gt2,
                  row(g_final), has_ctx, l == depth - 1)
    return xs[:, lc:] if has_ctx else xs
```

```python
import functools
import math

import jax
import jax.numpy as jnp
from jax import lax
from jax.experimental import pallas as pl
from jax.experimental.pallas import tpu as pltpu

EPS = 1e-6
GRID_W = 64
NA_HEADS, NA_DIM, NA_WIN_R, NA_WIN_C = 8, 64, 8, 16
A_W = NA_HEADS * NA_DIM
DN_HEADS, DN_DIM, DN_CHUNK = 4, 128, 64
DN_W = DN_HEADS * DN_DIM
ROPE_BASE = 10000.0
GM_GROUPS, GM_CHUNK = 8, 128

LANES = 128
TM = 256
DN_BATCH_UNROLL = 4
NA_ROWS_PER_STEP = 4
NA_CHAINS_IN_FLIGHT = 8
VMEM_LIMIT = 56 * 1024 * 1024
F32, BF16 = jnp.float32, jnp.bfloat16


def _cparams(n_axes):
    return pltpu.CompilerParams(dimension_semantics=("parallel",) * n_axes, vmem_limit_bytes=VMEM_LIMIT)


def _tok_spec(cols, off=0, col_block=0):
    return pl.BlockSpec((None, TM, cols), lambda b, j: (b, j + off, col_block))


def _mod_spec(d, ctx_tile):
    if ctx_tile:
        return pl.BlockSpec((None, 1, d), lambda b, j: (jnp.where(j == 0, 8, b), 0, 0))
    return pl.BlockSpec((None, 1, d), lambda b, j: (b, 0, 0))


def _full_spec(shape):
    n = len(shape)
    return pl.BlockSpec(shape, lambda b, j: (0,) * n)


def _norm_mod(x, g, sc, sh):
    y = x * lax.rsqrt(jnp.mean(x * x, axis=-1, keepdims=True) + EPS)
    return (y * g) * (1.0 + sc) + sh


def _silu(x):
    return x * jax.nn.sigmoid(x)


def _adaln_kernel(cond_ref, w_ref, b_ref, o_ref):
    a = _silu(cond_ref[...]).astype(BF16)
    o_ref[...] = jnp.dot(a, w_ref[...].astype(BF16), preferred_element_type=F32) + b_ref[...]


def _adaln(cond, w_ada, b_ada):
    depth, d, n = w_ada.shape
    tn = n // 4
    return pl.pallas_call(
        _adaln_kernel,
        grid=(depth, n // tn),
        in_specs=[pl.BlockSpec((16, d), lambda l, j: (0, 0)),
                  pl.BlockSpec((None, d, tn), lambda l, j: (l, 0, j)),
                  pl.BlockSpec((None, 1, tn), lambda l, j: (l, 0, j))],
        out_specs=pl.BlockSpec((None, 16, tn), lambda l, j: (l, 0, j)),
        out_shape=jax.ShapeDtypeStruct((depth, 16, n), F32),
        compiler_params=_cparams(2), name="adaln",
    )(cond, w_ada, b_ada.reshape(depth, 1, n))


def _even_in_kernel(x_ref, g_ref, sc_ref, sh_ref, w_ref, qa_ref, qb_ref, gate_ref, sm_ref):
    h = _norm_mod(x_ref[...], g_ref[...], sc_ref[...], sh_ref[...]).astype(BF16)
    p = jnp.dot(h, w_ref[...], preferred_element_type=F32)
    qa_ref[...] = p[:, :3 * A_W].astype(BF16)
    qb_ref[...] = p[:, 3 * A_W:3 * A_W + 3 * DN_W].astype(BF16)
    gate_ref[...] = p[:, 3 * A_W + 3 * DN_W:3 * A_W + 4 * DN_W].astype(BF16)
    sm_ref[...] = p[:, 3 * A_W + 4 * DN_W:]


def _even_in(x, g, sc, sh, w_pad):
    b, t, d = x.shape
    n = w_pad.shape[1]
    outs = [(3 * A_W, BF16), (3 * DN_W, BF16), (DN_W, BF16), (LANES, F32)]
    return pl.pallas_call(
        _even_in_kernel,
        grid=(b, t // TM),
        in_specs=[_tok_spec(d), _full_spec((1, d)), _mod_spec(d, True), _mod_spec(d, True),
                  _full_spec((d, n))],
        out_specs=[_tok_spec(c) for c, _ in outs],
        out_shape=[jax.ShapeDtypeStruct((b, t, c), dt) for c, dt in outs],
        compiler_params=_cparams(2), name="even_in",
    )(x, g, sc, sh, w_pad)


def _lockstep(gens):
    results = [None] * len(gens)
    live = list(range(len(gens)))
    while live:
        still = []
        for i in live:
            try:
                next(gens[i])
                still.append(i)
            except StopIteration as stop:
                results[i] = stop.value
        live = still
    return results


def _softmax_head_pair(q, keys, vals, biases):
    wq = q.shape[0]
    lane = lax.broadcasted_iota(jnp.int32, (2 * wq, LANES), 1)
    row = lax.broadcasted_iota(jnp.int32, (2 * wq, LANES), 0)
    own = (lane // NA_DIM) == (row // wq)
    q2 = jnp.concatenate([q, q], axis=0)
    qm = jnp.where(own, q2, jnp.zeros_like(q2))
    ss = [lax.dot_general(qm, kb, (((1,), (1,)), ((), ())), preferred_element_type=F32) for kb in keys]
    yield
    ss = [s if bias is None else s + bias for s, bias in zip(ss, biases)]
    m = functools.reduce(jnp.maximum, [jnp.max(s, axis=-1, keepdims=True) for s in ss])
    yield
    ps = [jnp.exp(s - m) for s in ss]
    den = functools.reduce(jnp.add, [jnp.sum(p, axis=-1, keepdims=True) for p in ps])
    yield
    o = functools.reduce(jnp.add, [jnp.dot(p.astype(BF16), vb, preferred_element_type=F32)
                                   for p, vb in zip(ps, vals)])
    yield
    o = o / den
    return jnp.where(lax.broadcasted_iota(jnp.int32, (wq, LANES), 1) < NA_DIM, o[:wq], o[wq:])


def _attend_jobs(q_ref, o_ref, jobs):
    scale = NA_DIM ** -0.5
    gens, dests = [], []
    for rsl, blocks, bias_of in jobs:
        for pair in range(NA_HEADS // 2):
            cs = slice(pair * LANES, (pair + 1) * LANES)
            keys, vals = blocks(cs)
            gens.append(_softmax_head_pair(q_ref[rsl, cs] * scale, keys, vals, bias_of(pair)))
            dests.append((rsl, cs))
    for g0 in range(0, len(gens), NA_CHAINS_IN_FLIGHT):
        outs = _lockstep(gens[g0:g0 + NA_CHAINS_IN_FLIGHT])
        for (rsl, cs), o in zip(dests[g0:g0 + NA_CHAINS_IN_FLIGHT], outs):
            o_ref[rsl, cs] = o.astype(o_ref.dtype)


def _na_kernel(q_ref, k_ref, v_ref, bias_ref, o_ref, *, lc, rows, ctx_steps):
    step = pl.program_id(1)
    n_loc = NA_WIN_R * GRID_W

    def latent_rows():
        jobs = []
        for i in range(NA_ROWS_PER_STEP):
            r = (step - ctx_steps) * NA_ROWS_PER_STEP + i
            rs = jnp.clip(r - NA_WIN_R // 2, 0, rows - NA_WIN_R)
            start = pl.multiple_of(lc + rs * GRID_W, GRID_W)
            blocks = lambda cs, start=start: ([k_ref[pl.ds(start, n_loc), cs], k_ref[0:lc, cs]],
                                              [v_ref[pl.ds(start, n_loc), cs], v_ref[0:lc, cs]])
            bias_of = lambda pair, var=r - rs: [bias_ref[var, 2 * pair:2 * pair + 2].reshape(2 * GRID_W, n_loc), None]
            jobs.append((slice(i * GRID_W, (i + 1) * GRID_W), blocks, bias_of))
        _attend_jobs(q_ref, o_ref, jobs)

    def context_block():
        blocks = lambda cs: ([k_ref[0:lc, cs]], [v_ref[0:lc, cs]])
        _attend_jobs(q_ref, o_ref, [(slice(None), blocks, lambda pair: [None])])

    if ctx_steps:
        pl.when(step < ctx_steps)(context_block)
        pl.when(step >= ctx_steps)(latent_rows)
    else:
        latent_rows()


def _na_bias_table(rpb):
    col = jnp.arange(GRID_W)
    cstart = jnp.clip(col - NA_WIN_C // 2, 0, GRID_W - NA_WIN_C)
    col_ok = (col[None, :] >= cstart[:, None]) & (col[None, :] < cstart[:, None] + NA_WIN_C)
    dc_idx = jnp.clip(col[None, :] - col[:, None], -(NA_WIN_C - 1), NA_WIN_C - 1) + NA_WIN_C - 1
    onehot = (dc_idx[:, :, None] == jnp.arange(2 * NA_WIN_C - 1)).astype(F32)
    full = jnp.einsum('hij,qkj->hqik', rpb.astype(F32), onehot, precision=lax.Precision.HIGHEST)
    full = jnp.where(col_ok[None, :, None, :], full, -jnp.inf)
    tabs = [full[:, :, NA_WIN_R - 1 - v:2 * NA_WIN_R - 1 - v].reshape(NA_HEADS, GRID_W, NA_WIN_R * GRID_W)
            for v in range(NA_WIN_R)]
    return jnp.stack(tabs)


def _na_attention(qkv_a, bias_tab, lc, ctx_out):
    b, t, _ = qkv_a.shape
    rows = (t - lc) // GRID_W
    tq = NA_ROWS_PER_STEP * GRID_W
    assert rows >= NA_WIN_R and rows % NA_ROWS_PER_STEP == 0 and lc % tq == 0
    ctx_steps = lc // tq if ctx_out else 0
    q_off = 0 if ctx_out else lc // tq
    n_out = t if ctx_out else t - lc
    return pl.pallas_call(
        functools.partial(_na_kernel, lc=lc, rows=rows, ctx_steps=ctx_steps),
        grid=(b, ctx_steps + rows // NA_ROWS_PER_STEP),
        in_specs=[pl.BlockSpec((None, tq, A_W), lambda bi, s: (bi, s + q_off, 0)),
                  pl.BlockSpec((None, t, A_W), lambda bi, s: (bi, 0, 1)),
                  pl.BlockSpec((None, t, A_W), lambda bi, s: (bi, 0, 2)),
                  pl.BlockSpec(bias_tab.shape, lambda bi, s: (0, 0, 0, 0))],
        out_specs=pl.BlockSpec((None, tq, A_W), lambda bi, s: (bi, s, 0)),
        out_shape=jax.ShapeDtypeStruct((b, n_out, A_W), BF16),
        compiler_params=_cparams(2), name="na_attention",
    )(qkv_a, qkv_a, qkv_a, bias_tab)


def _chunk_scan(x, reverse):
    n = x.shape[0]
    pos = lax.broadcasted_iota(jnp.int32, x.shape, 0) % DN_CHUNK
    s = 1
    while s < DN_CHUNK:
        if reverse:
            x = x + jnp.where(pos < DN_CHUNK - s, pltpu.roll(x, n - s, 0), 0.0)
        else:
            x = x + jnp.where(pos >= s, pltpu.roll(x, s, 0), 0.0)
        s *= 2
    return x


def _dn_prep_kernel(x_ref, prev_ref, next_ref, sm_ref, cw_ref, cos_ref, sina_ref, sinb_ref, alog_ref, dtb_ref,
                    q_ref, k_ref, v_ref, col_ref, row_ref, *, n_tiles):
    j = pl.program_id(1)
    x = x_ref[...].astype(F32)
    row = lax.broadcasted_iota(jnp.int32, x.shape, 0)
    has_prev = jnp.logical_and(j != 0, j != 1)
    has_next = jnp.logical_and(j != 0, j != n_tiles - 1)
    halo_p = jnp.where(has_prev, prev_ref[15:16, :].astype(F32), 0.0)
    halo_n = jnp.where(has_next, next_ref[0:1, :].astype(F32), 0.0)
    xm = jnp.where(row == 0, halo_p, pltpu.roll(x, 1, 0))
    xp = jnp.where(row == TM - 1, halo_n, pltpu.roll(x, TM - 1, 0))
    y = _silu(cw_ref[0:1, :] * xm + cw_ref[1:2, :] * x + cw_ref[2:3, :] * xp)

    cos, sina, sinb = cos_ref[...], sina_ref[...], sinb_ref[...]
    for h in range(DN_HEADS):
        for base, o_ref, scale in ((0, q_ref, DN_DIM ** -0.5), (DN_W, k_ref, None)):
            cs = slice(base + h * DN_DIM, base + (h + 1) * DN_DIM)
            z = y[:, cs]
            z = z * lax.rsqrt(jnp.sum(z * z, axis=-1, keepdims=True) + EPS)
            z = z * cos + pltpu.roll(z, 3 * DN_DIM // 4, 1) * sina + pltpu.roll(z, DN_DIM // 4, 1) * sinb
            if scale is not None:
                z = z * scale
            o_ref[:, h * DN_DIM:(h + 1) * DN_DIM] = z.astype(o_ref.dtype)
    v_ref[...] = y[:, 2 * DN_W:].astype(v_ref.dtype)

    sm = sm_ref[...]
    lane = lax.broadcasted_iota(jnp.int32, sm.shape, 1)
    logg = -jnp.exp(alog_ref[...]) * jax.nn.softplus(sm + dtb_ref[...])
    gam = jnp.where(lane < 3 * DN_HEADS, _chunk_scan(logg, False), _chunk_scan(logg, True))
    col = jnp.where(lane < 2 * DN_HEADS, jax.nn.sigmoid(sm), gam)
    col_ref[...] = col
    row_ref[...] = col.T[0:4 * DN_HEADS, :]


def _rope_tables(lc, seq):
    t = jnp.arange(seq)
    rowp = (t // GRID_W).astype(F32)
    colp = (t % GRID_W).astype(F32)
    n_freq = DN_DIM // 4
    inv = ROPE_BASE ** (-jnp.arange(n_freq, dtype=F32) / n_freq)
    ar, ac = rowp[:, None] * inv, colp[:, None] * inv
    ang = jnp.concatenate([ar, ar, ac, ac], axis=-1)
    cos, sin = jnp.cos(ang), jnp.sin(ang)
    quarter = (jnp.arange(DN_DIM) // n_freq) % 2
    sina = jnp.where(quarter == 0, -sin, 0.0)
    sinb = jnp.where(quarter == 1, sin, 0.0)
    pad = lambda a, v: jnp.concatenate([jnp.full((lc, DN_DIM), v, F32), a], axis=0)
    return pad(cos, 1.0), pad(sina, 0.0), pad(sinb, 0.0)


def _dn_prep(qkv_b, small, conv_w, a_log, dt_bias, rope):
    b, t, _ = qkv_b.shape
    n_tiles = t // TM
    hb = TM // 16
    pad16 = lambda a: jnp.zeros((1, LANES), F32).at[0, 2 * DN_HEADS:4 * DN_HEADS].set(a.reshape(-1).astype(F32))
    outs = [(DN_W, BF16), (DN_W, BF16), (DN_W, BF16), (LANES, F32)]
    res = pl.pallas_call(
        functools.partial(_dn_prep_kernel, n_tiles=n_tiles),
        grid=(b, n_tiles),
        in_specs=[_tok_spec(3 * DN_W),
                  pl.BlockSpec((None, 16, 3 * DN_W), lambda bi, j: (bi, jnp.maximum(j * hb - 1, 0), 0)),
                  pl.BlockSpec((None, 16, 3 * DN_W), lambda bi, j: (bi, jnp.minimum((j + 1) * hb, t // 16 - 1), 0)),
                  _tok_spec(LANES),
                  _full_spec((3, 3 * DN_W)),
                  pl.BlockSpec((TM, DN_DIM), lambda bi, j: (j, 0)),
                  pl.BlockSpec((TM, DN_DIM), lambda bi, j: (j, 0)),
                  pl.BlockSpec((TM, DN_DIM), lambda bi, j: (j, 0)),
                  _full_spec((1, LANES)), _full_spec((1, LANES))],
        out_specs=[_tok_spec(c) for c, _ in outs] + [pl.BlockSpec((None, 4 * DN_HEADS, TM), lambda bi, j: (bi, 0, j))],
        out_shape=[jax.ShapeDtypeStruct((b, t, c), dt) for c, dt in outs]
        + [jax.ShapeDtypeStruct((b, 4 * DN_HEADS, t), F32)],
        compiler_params=_cparams(2), name="dn_prep",
    )(qkv_b, qkv_b, qkv_b, small, conv_w.astype(F32), *rope, pad16(a_log), pad16(dt_bias))
    q, k, v, colsm, rowsm = res
    n_chunks = t // DN_CHUNK
    rowsm = rowsm.reshape(b, 4 * DN_HEADS, n_chunks, DN_CHUNK).transpose(2, 0, 1, 3)
    rowsm = rowsm.reshape(n_chunks, b, 2 * DN_HEADS, 2 * DN_CHUNK)
    return q, k, v, colsm, rowsm


def _block_diag(x):
    n, w = x.shape
    x2 = jnp.concatenate([x, x], axis=0)
    row = lax.broadcasted_iota(jnp.int32, x2.shape, 0)
    lane = lax.broadcasted_iota(jnp.int32, x2.shape, 1)
    return jnp.where((row // n) == (lane // (w // 2)), x2, jnp.zeros_like(x2))


def _dn_pair_step(d, p, q_all, k_all, v_all, cm, rm, s_prev):
    c = DN_CHUNK
    heads = (2 * p, 2 * p + 1)
    cs2 = slice(2 * p * DN_DIM, (2 * p + 2) * DN_DIM)
    q2, k2, v2 = q_all[:, cs2], k_all[:, cs2], v_all[:, cs2]
    lane = lax.broadcasted_iota(jnp.int32, (c, 2 * c), 1)
    ri = lax.broadcasted_iota(jnp.int32, (c, 2 * c), 0)
    ci = lane % c
    first = lane < c
    col = lambda base: [cm[:, base + h:base + h + 1] for h in heads]
    beta, gam_c = col(d * DN_HEADS), col(2 * DN_HEADS + d * DN_HEADS)
    side = lambda ab: jnp.where(first, ab[0], ab[1])
    ir = (2 * DN_HEADS + d * DN_HEADS) // 2 + p
    gam_r = rm[ir:ir + 1, :]
    if d == 0:
        incl, strict, g_last = ri >= ci, ri > ci, [g[c - 1:c, :] for g in gam_c]
    else:
        incl, strict, g_last = ri <= ci, ri < ci, [g[0:1, :] for g in gam_c]
    decay = jnp.exp(jnp.where(incl, side(gam_c) - gam_r, -jnp.inf))
    zk = jnp.zeros((c, DN_DIM), BF16)
    kd = jnp.concatenate([jnp.concatenate([k2[:, :DN_DIM], zk], axis=1),
                          jnp.concatenate([zk, k2[:, DN_DIM:]], axis=1)], axis=0)
    g = lax.dot_general(jnp.concatenate([k2, q2], axis=0), kd, (((1,), (1,)), ((), ())),
                        preferred_element_type=F32)
    yield
    kk, qk = g[:c], g[c:]
    lneg = jnp.where(strict, -(side(beta) * kk * decay), 0.0)
    aqk = qk * decay
    eg = [jnp.exp(gc) for gc in gam_c]
    kf = [k2[:, i * DN_DIM:(i + 1) * DN_DIM].astype(F32) for i in range(2)]
    vf = [v2[:, i * DN_DIM:(i + 1) * DN_DIM].astype(F32) for i in range(2)]
    qf = [q2[:, i * DN_DIM:(i + 1) * DN_DIM].astype(F32) for i in range(2)]
    rhs = [jnp.concatenate([(beta[i] * eg[i]) * kf[i], beta[i] * vf[i]], axis=1).astype(BF16) for i in range(2)]
    eye = (ri == ci).astype(F32)
    same_blk = lambda s: (ri // s) == (ci // s)
    dneg = jnp.where(same_blk(4), lneg, 0.0)
    dnb = dneg.astype(BF16)
    dsq = jnp.dot(dnb, _block_diag(dnb), preferred_element_type=F32)
    yield
    t = (eye + dneg) + jnp.dot((eye + dneg).astype(BF16), _block_diag(dsq.astype(BF16)), preferred_element_type=F32)
    yield
    s = 4
    while s < c:
        eneg = jnp.where(jnp.logical_and(same_blk(2 * s), jnp.logical_not(same_blk(s))), lneg, 0.0)
        tb = t.astype(BF16)
        te = jnp.dot(tb, _block_diag(eneg.astype(BF16)), preferred_element_type=F32)
        yield
        t = t + jnp.dot(te.astype(BF16), _block_diag(tb), preferred_element_type=F32)
        yield
        s *= 2
    sol = jnp.dot(_block_diag(t.astype(BF16)), jnp.concatenate(rhs, axis=0), preferred_element_type=F32).astype(BF16)
    yield
    kdec = jnp.concatenate([kf[i] * jnp.exp(g_last[i] - gam_c[i]) for i in range(2)], axis=0)
    lhs = jnp.concatenate([_block_diag(aqk.astype(BF16)), _block_diag(kdec.T.astype(BF16))], axis=0)
    res = jnp.dot(lhs, sol, preferred_element_type=F32)
    yield
    outs, states = [], []
    for i in range(2):
        qo = res[i * c:(i + 1) * c]
        ab = res[2 * c + i * DN_DIM:2 * c + (i + 1) * DN_DIM]
        qeff = qf[i] * eg[i] - qo[:, :DN_DIM]
        both = jnp.dot(jnp.concatenate([qeff, ab[:, :DN_DIM]], axis=0).astype(BF16), s_prev[i].astype(BF16),
                       preferred_element_type=F32)
        outs.append(both[:c] + qo[:, DN_DIM:])
        states.append(jnp.exp(g_last[i]) * s_prev[i] - both[c:] + ab[:, DN_DIM:])
    return outs, states


def _dn_chain_kernel(qf_ref, kf_ref, vf_ref, cf_ref, rf_ref, qb_ref, kb_ref, vb_ref, cb_ref, rb_ref,
                     of_ref, ob_ref, s_ref, *, nb):
    @pl.when(pl.program_id(0) == 0)
    def _():
        s_ref[...] = jnp.zeros_like(s_ref)

    in_refs = ((qf_ref, kf_ref, vf_ref, cf_ref, rf_ref), (qb_ref, kb_ref, vb_ref, cb_ref, rb_ref))
    per_iter = DN_BATCH_UNROLL if nb % DN_BATCH_UNROLL == 0 else 1

    def body(i, carry):
        bs = [i * per_iter + u for u in range(per_iter)]
        n_pairs = DN_HEADS // 2
        gens = [_dn_pair_step(d, p, *[r[b] for r in in_refs[d]], [s_ref[d, b, 2 * p], s_ref[d, b, 2 * p + 1]])
                for b in bs for d in range(2) for p in range(n_pairs)]
        res = _lockstep(gens)
        for ib, b in enumerate(bs):
            for d, o_r in enumerate((of_ref, ob_ref)):
                base = (ib * 2 + d) * n_pairs
                o_r[b] = jnp.concatenate([o for p in range(n_pairs) for o in res[base + p][0]], axis=1)
                for p in range(n_pairs):
                    for i in range(2):
                        s_ref[d, b, 2 * p + i] = res[base + p][1][i]
        return carry

    lax.fori_loop(0, nb // per_iter, body, 0)


def _dn_chain(q, k, v, colsm, rowsm, lc):
    b, t, _ = q.shape
    n_chunks = t // DN_CHUNK
    nc_ctx = lc // DN_CHUNK
    fwd = lambda i: i
    bwd = lambda i: jnp.where(i < nc_ctx, nc_ctx - 1 - i, n_chunks - 1 + nc_ctx - i)
    tok = lambda cols, cm: pl.BlockSpec((b, DN_CHUNK, cols), lambda i: (0, cm(i), 0))
    rowspec = lambda cm: pl.BlockSpec((None, b, 2 * DN_HEADS, 2 * DN_CHUNK), lambda i: (cm(i), 0, 0, 0))
    in_specs = []
    for cm in (fwd, bwd):
        in_specs += [tok(DN_W, cm), tok(DN_W, cm), tok(DN_W, cm), tok(LANES, cm), rowspec(cm)]
    return pl.pallas_call(
        functools.partial(_dn_chain_kernel, nb=b),
        grid=(n_chunks,),
        in_specs=in_specs,
        out_specs=[tok(DN_W, fwd), tok(DN_W, bwd)],
        out_shape=[jax.ShapeDtypeStruct((b, t, DN_W), F32)] * 2,
        scratch_shapes=[pltpu.VMEM((2, b, DN_HEADS, DN_DIM, DN_DIM), F32)],
        compiler_params=pltpu.CompilerParams(dimension_semantics=("arbitrary",), vmem_limit_bytes=VMEM_LIMIT),
        name="dn_chain",
    )(q, k, v, colsm, rowsm, q, k, v, colsm, rowsm)


def _even_out_kernel(x_ref, att_ref, of_ref, ob_ref, gate_ref, gout_ref, w_ref, gt_ref, o_ref):
    o = of_ref[...] + ob_ref[...]
    gate = gate_ref[...].astype(F32)
    parts = [att_ref[...]]
    for h in range(DN_HEADS):
        cs = slice(h * DN_DIM, (h + 1) * DN_DIM)
        oh = o[:, cs]
        yh = oh * lax.rsqrt(jnp.mean(oh * oh, axis=-1, keepdims=True) + EPS) * gout_ref[...]
        parts.append((yh * _silu(gate[:, cs])).astype(BF16))
    a = jnp.concatenate(parts, axis=1)
    y = jnp.dot(a, w_ref[...], preferred_element_type=F32)
    o_ref[...] = x_ref[...] + gt_ref[...] * y


def _even_out(x, att, o_f, o_b, gate, g_out, w_out, gt, ctx_out):
    b, t, d = x.shape
    off = 0 if ctx_out else 1
    n_t = t // TM - off
    return pl.pallas_call(
        _even_out_kernel,
        grid=(b, n_t),
        in_specs=[_tok_spec(d, off), _tok_spec(A_W), _tok_spec(DN_W, off), _tok_spec(DN_W, off), _tok_spec(DN_W, off),
                  _full_spec((1, DN_DIM)), _full_spec((A_W + DN_W, d)), _mod_spec(d, ctx_out)],
        out_specs=_tok_spec(d),
        out_shape=jax.ShapeDtypeStruct((b, n_t * TM, d), F32),
        compiler_params=_cparams(2), name="even_out",
    )(x, att, o_f, o_b, gate, g_out, w_out, gt)


def _gelu(x):
    return 0.5 * x * (1.0 + lax.erf(x * (2.0 ** -0.5)))


def _skewed(gens):
    results = [None] * len(gens)
    live, started = [], 0
    while started < len(gens) or live:
        if started < len(gens):
            live.append(started)
            started += 1
        still = []
        for i in reversed(live):
            try:
                next(gens[i])
                still.append(i)
            except StopIteration as stop:
                results[i] = stop.value
        live = still[::-1]
    return results


def _gmlp_kernel(x_ref, g_ref, sc_ref, sh_ref, wi_ref, gv_ref, ws_ref, bs_ref, wo_ref, gt_ref, o_ref, u_ref, v_ref):
    x = x_ref[...]
    h = _norm_mod(x, g_ref[...], sc_ref[...], sh_ref[...]).astype(BF16)
    half = wi_ref.shape[1] // 2
    gw = half // GM_GROUPS

    pw = 2 * gw
    n_pairs = GM_GROUPS // 2

    def project(c0, dst_ref, want_ssq):
        z = jnp.dot(h, wi_ref[:, c0:c0 + pw], preferred_element_type=F32)
        yield
        z = _gelu(z)
        d0 = c0 % half
        dst_ref[:, d0:d0 + pw] = z.astype(dst_ref.dtype)
        return jnp.sum(z * z, axis=-1, keepdims=True) if want_ssq else None

    cols = [(half + p * pw, v_ref, True) for p in range(n_pairs)] + [(p * pw, u_ref, False) for p in range(n_pairs)]
    ssq = functools.reduce(jnp.add, _skewed([project(*c) for c in cols])[:n_pairs])
    rinv = lax.rsqrt(ssq * (1.0 / half) + EPS)

    def mix(p):
        cs = slice(p * pw, (p + 1) * pw)
        vb = (v_ref[:, cs] * rinv * gv_ref[:, cs]).astype(BF16)
        mixed = [[jnp.dot(ws_ref[2 * p + k], vb[n * GM_CHUNK:(n + 1) * GM_CHUNK, k * gw:(k + 1) * gw],
                          preferred_element_type=F32) for k in range(2)] for n in range(TM // GM_CHUNK)]
        yield
        bias = [jnp.concatenate([bs_ref[2 * p + k]] * (gw // LANES), axis=1) for k in range(2)]
        t = jnp.concatenate([jnp.concatenate([row[k] + bias[k] for k in range(2)], axis=1) for row in mixed], axis=0)
        t = (t * u_ref[:, cs]).astype(BF16)
        return jnp.dot(t, wo_ref[cs, :], preferred_element_type=F32)

    y = functools.reduce(jnp.add, _skewed([mix(p) for p in range(n_pairs)]))
    o_ref[...] = x + gt_ref[...] * y


def _gmlp(x, g, sc, sh, w_in, g_v, ws, bs, w_out, gt, has_ctx):
    b, t, d = x.shape
    half = w_in.shape[1] // 2
    assert (half // GM_GROUPS) % LANES == 0
    return pl.pallas_call(
        _gmlp_kernel,
        grid=(b, t // TM),
        in_specs=[_tok_spec(d), _full_spec((1, d)), _mod_spec(d, has_ctx), _mod_spec(d, has_ctx),
                  _full_spec(w_in.shape), _full_spec((1, half)), _full_spec(ws.shape), _full_spec(bs.shape),
                  _full_spec(w_out.shape), _mod_spec(d, has_ctx)],
        out_specs=_tok_spec(d),
        out_shape=jax.ShapeDtypeStruct((b, t, d), F32),
        scratch_shapes=[pltpu.VMEM((TM, half), F32), pltpu.VMEM((TM, half), F32)],
        compiler_params=_cparams(2), name="gmlp",
    )(x, g, sc, sh, w_in, g_v, ws, bs, w_out, gt)


def _ffn_kernel(x_ref, g_ref, sc_ref, sh_ref, w1_ref, w2_ref, gt_ref, gf_ref, o_ref, *, final_norm):
    x = x_ref[...]
    h = _norm_mod(x, g_ref[...], sc_ref[...], sh_ref[...]).astype(BF16)
    a = jnp.maximum(jnp.dot(h, w1_ref[...], preferred_element_type=F32), 0.0)
    y = jnp.dot((a * a).astype(BF16), w2_ref[...], preferred_element_type=F32)
    out = x + gt_ref[...] * y
    if final_norm:
        out = out * lax.rsqrt(jnp.mean(out * out, axis=-1, keepdims=True) + EPS) * gf_ref[...]
    o_ref[...] = out


def _ffn(x, g, sc, sh, w1, w2, gt, g_final, has_ctx, final_norm):
    b, t, d = x.shape
    return pl.pallas_call(
        functools.partial(_ffn_kernel, final_norm=final_norm),
        grid=(b, t // TM),
        in_specs=[_tok_spec(d), _full_spec((1, d)), _mod_spec(d, has_ctx), _mod_spec(d, has_ctx),
                  _full_spec(w1.shape), _full_spec(w2.shape), _mod_spec(d, has_ctx), _full_spec((1, d))],
        out_specs=_tok_spec(d),
        out_shape=jax.ShapeDtypeStruct((b, t, d), F32),
        compiler_params=_cparams(2), name="ffn",
    )(x, g, sc, sh, w1, w2, gt, g_final)


def kernel(x, c, ctx, c_ctx, w_ada, b_ada, g_norm_mix, g_norm_ffn, w_in_even, w_out_even, na_rpb, dn_conv, dn_a_log,
           dn_dt_bias, dn_g_out, w_in_odd, gm_g_v, gm_ws, gm_bs, w_out_odd, w_ff1, w_ff2, g_final):
    nb, seq, d = x.shape
    lc = ctx.shape[1]
    depth = w_ada.shape[0]
    assert nb <= 8 and seq % TM == 0 and lc == TM and seq % GRID_W == 0

    cond = jnp.zeros((16, d), F32).at[:nb].set(c).at[8].set(c_ctx)
    mods = _adaln(cond, w_ada, b_ada).reshape(depth, 16, 6, 1, d)
    rope = _rope_tables(lc, seq)
    row = lambda a: a.reshape(1, -1).astype(F32)

    xs = jnp.concatenate([ctx, x], axis=1)
    has_ctx = True
    for l in range(depth):
        ctx_live = any(j % 2 == 0 for j in range(l + 1, depth))
        sh1, sc1, gt1, sh2, sc2, gt2 = [mods[l, :, i] for i in range(6)]
        if l % 2 == 0:
            assert has_ctx
            e = l // 2
            w_pad = jnp.pad(w_in_even[e], ((0, 0), (0, LANES - 4 * DN_HEADS))).astype(BF16)
            qkv_a, qkv_b, gate, small = _even_in(xs, row(g_norm_mix[l]), sc1, sh1, w_pad)
            att = _na_attention(qkv_a, _na_bias_table(na_rpb[e]), lc, ctx_live)
            q, k, v, colsm, rowsm = _dn_prep(qkv_b, small, dn_conv[e], dn_a_log[e], dn_dt_bias[e], rope)
            o_f, o_b = _dn_chain(q, k, v, colsm, rowsm, lc)
            xs = _even_out(xs, att, o_f, o_b, gate, row(dn_g_out[e]), w_out_even[e].astype(BF16), gt1, ctx_live)
            has_ctx = ctx_live
        else:
            o = l // 2
            bs = jnp.broadcast_to(gm_bs[o].astype(F32)[:, :, None], (GM_GROUPS, GM_CHUNK, LANES))
            xs = _gmlp(xs, row(g_norm_mix[l]), sc1, sh1, w_in_odd[o].astype(BF16), row(gm_g_v[o]),
                       gm_ws[o].astype(BF16), bs, w_out_odd[o].astype(BF16), gt1, has_ctx)
        xs = _ffn(xs, row(g_norm_ffn[l]), sc2, sh2, w_ff1[l].astype(BF16), w_ff2[l].astype(BF16), gt2,
                  row(g_final), has_ctx, l == depth - 1)
    return xs[:, lc:] if has_ctx else xs
```

```python
import functools
import math

import jax
import jax.numpy as jnp
from jax import lax
from jax.experimental import pallas as pl
from jax.experimental.pallas import tpu as pltpu

EPS = 1e-6
GRID_W = 64
NA_HEADS, NA_DIM, NA_WIN_R, NA_WIN_C = 8, 64, 8, 16
A_W = NA_HEADS * NA_DIM
DN_HEADS, DN_DIM, DN_CHUNK = 4, 128, 64
DN_W = DN_HEADS * DN_DIM
ROPE_BASE = 10000.0
GM_GROUPS, GM_CHUNK = 8, 128

LANES = 128
TM = 256
DN_BATCH_UNROLL = 4
NA_ROWS_PER_STEP = 4
NA_CHAINS_IN_FLIGHT = 8
VMEM_LIMIT = 56 * 1024 * 1024
F32, BF16 = jnp.float32, jnp.bfloat16


def _cparams(n_axes):
    return pltpu.CompilerParams(dimension_semantics=("parallel",) * n_axes, vmem_limit_bytes=VMEM_LIMIT)


def _tok_spec(cols, off=0, col_block=0):
    return pl.BlockSpec((None, TM, cols), lambda b, j: (b, j + off, col_block))


def _mod_spec(d, ctx_tile):
    if ctx_tile:
        return pl.BlockSpec((None, 1, d), lambda b, j: (jnp.where(j == 0, 8, b), 0, 0))
    return pl.BlockSpec((None, 1, d), lambda b, j: (b, 0, 0))


def _full_spec(shape):
    n = len(shape)
    return pl.BlockSpec(shape, lambda b, j: (0,) * n)


def _norm_mod(x, g, sc, sh):
    y = x * lax.rsqrt(jnp.mean(x * x, axis=-1, keepdims=True) + EPS)
    return (y * g) * (1.0 + sc) + sh


def _silu(x):
    return x * jax.nn.sigmoid(x)


def _adaln_kernel(cond_ref, w_ref, b_ref, o_ref):
    a = _silu(cond_ref[...]).astype(BF16)
    o_ref[...] = jnp.dot(a, w_ref[...].astype(BF16), preferred_element_type=F32) + b_ref[...]


def _adaln(cond, w_ada, b_ada):
    depth, d, n = w_ada.shape
    tn = n // 4
    return pl.pallas_call(
        _adaln_kernel,
        grid=(depth, n // tn),
        in_specs=[pl.BlockSpec((16, d), lambda l, j: (0, 0)),
                  pl.BlockSpec((None, d, tn), lambda l, j: (l, 0, j)),
                  pl.BlockSpec((None, 1, tn), lambda l, j: (l, 0, j))],
        out_specs=pl.BlockSpec((None, 16, tn), lambda l, j: (l, 0, j)),
        out_shape=jax.ShapeDtypeStruct((depth, 16, n), F32),
        compiler_params=_cparams(2), name="adaln",
    )(cond, w_ada, b_ada.reshape(depth, 1, n))


def _even_in_kernel(x_ref, g_ref, sc_ref, sh_ref, w_ref, qa_ref, qb_ref, gate_ref, sm_ref):
    h = _norm_mod(x_ref[...], g_ref[...], sc_ref[...], sh_ref[...]).astype(BF16)
    p = jnp.dot(h, w_ref[...], preferred_element_type=F32)
    qa_ref[...] = p[:, :3 * A_W].astype(BF16)
    qb_ref[...] = p[:, 3 * A_W:3 * A_W + 3 * DN_W].astype(BF16)
    gate_ref[...] = p[:, 3 * A_W + 3 * DN_W:3 * A_W + 4 * DN_W].astype(BF16)
    sm_ref[...] = p[:, 3 * A_W + 4 * DN_W:]


def _even_in(x, g, sc, sh, w_pad):
    b, t, d = x.shape
    n = w_pad.shape[1]
    outs = [(3 * A_W, BF16), (3 * DN_W, BF16), (DN_W, BF16), (LANES, F32)]
    return pl.pallas_call(
        _even_in_kernel,
        grid=(b, t // TM),
        in_specs=[_tok_spec(d), _full_spec((1, d)), _mod_spec(d, True), _mod_spec(d, True),
                  _full_spec((d, n))],
        out_specs=[_tok_spec(c) for c, _ in outs],
        out_shape=[jax.ShapeDtypeStruct((b, t, c), dt) for c, dt in outs],
        compiler_params=_cparams(2), name="even_in",
    )(x, g, sc, sh, w_pad)


def _lockstep(gens):
    results = [None] * len(gens)
    live = list(range(len(gens)))
    while live:
        still = []
        for i in live:
            try:
                next(gens[i])
                still.append(i)
            except StopIteration as stop:
                results[i] = stop.value
        live = still
    return results


def _softmax_head_pair(q, keys, vals, biases):
    wq = q.shape[0]
    lane = lax.broadcasted_iota(jnp.int32, (2 * wq, LANES), 1)
    row = lax.broadcasted_iota(jnp.int32, (2 * wq, LANES), 0)
    own = (lane // NA_DIM) == (row // wq)
    q2 = jnp.concatenate([q, q], axis=0)
    qm = jnp.where(own, q2, jnp.zeros_like(q2))
    ss = [lax.dot_general(qm, kb, (((1,), (1,)), ((), ())), preferred_element_type=F32) for kb in keys]
    yield
    ss = [s if bias is None else s + bias for s, bias in zip(ss, biases)]
    m = functools.reduce(jnp.maximum, [jnp.max(s, axis=-1, keepdims=True) for s in ss])
    yield
    ps = [jnp.exp(s - m) for s in ss]
    den = functools.reduce(jnp.add, [jnp.sum(p, axis=-1, keepdims=True) for p in ps])
    yield
    o = functools.reduce(jnp.add, [jnp.dot(p.astype(BF16), vb, preferred_element_type=F32)
                                   for p, vb in zip(ps, vals)])
    yield
    o = o / den
    return jnp.where(lax.broadcasted_iota(jnp.int32, (wq, LANES), 1) < NA_DIM, o[:wq], o[wq:])


def _attend_jobs(q_ref, o_ref, jobs):
    scale = NA_DIM ** -0.5
    gens, dests = [], []
    for rsl, blocks, bias_of in jobs:
        for pair in range(NA_HEADS // 2):
            cs = slice(pair * LANES, (pair + 1) * LANES)
            keys, vals = blocks(cs)
            gens.append(_softmax_head_pair(q_ref[rsl, cs] * scale, keys, vals, bias_of(pair)))
            dests.append((rsl, cs))
    for g0 in range(0, len(gens), NA_CHAINS_IN_FLIGHT):
        outs = _lockstep(gens[g0:g0 + NA_CHAINS_IN_FLIGHT])
        for (rsl, cs), o in zip(dests[g0:g0 + NA_CHAINS_IN_FLIGHT], outs):
            o_ref[rsl, cs] = o.astype(o_ref.dtype)


def _na_kernel(q_ref, k_ref, v_ref, bias_ref, o_ref, *, lc, rows, ctx_steps):
    step = pl.program_id(1)
    n_loc = NA_WIN_R * GRID_W

    def latent_rows():
        jobs = []
        for i in range(NA_ROWS_PER_STEP):
            r = (step - ctx_steps) * NA_ROWS_PER_STEP + i
            rs = jnp.clip(r - NA_WIN_R // 2, 0, rows - NA_WIN_R)
            start = pl.multiple_of(lc + rs * GRID_W, GRID_W)
            blocks = lambda cs, start=start: ([k_ref[pl.ds(start, n_loc), cs], k_ref[0:lc, cs]],
                                              [v_ref[pl.ds(start, n_loc), cs], v_ref[0:lc, cs]])
            bias_of = lambda pair, var=r - rs: [bias_ref[var, 2 * pair:2 * pair + 2].reshape(2 * GRID_W, n_loc), None]
            jobs.append((slice(i * GRID_W, (i + 1) * GRID_W), blocks, bias_of))
        _attend_jobs(q_ref, o_ref, jobs)

    def context_block():
        blocks = lambda cs: ([k_ref[0:lc, cs]], [v_ref[0:lc, cs]])
        _attend_jobs(q_ref, o_ref, [(slice(None), blocks, lambda pair: [None])])

    if ctx_steps:
        pl.when(step < ctx_steps)(context_block)
        pl.when(step >= ctx_steps)(latent_rows)
    else:
        latent_rows()


def _na_bias_table(rpb):
    col = jnp.arange(GRID_W)
    cstart = jnp.clip(col - NA_WIN_C // 2, 0, GRID_W - NA_WIN_C)
    col_ok = (col[None, :] >= cstart[:, None]) & (col[None, :] < cstart[:, None] + NA_WIN_C)
    dc_idx = jnp.clip(col[None, :] - col[:, None], -(NA_WIN_C - 1), NA_WIN_C - 1) + NA_WIN_C - 1
    onehot = (dc_idx[:, :, None] == jnp.arange(2 * NA_WIN_C - 1)).astype(F32)
    full = jnp.einsum('hij,qkj->hqik', rpb.astype(F32), onehot, precision=lax.Precision.HIGHEST)
    full = jnp.where(col_ok[None, :, None, :], full, -jnp.inf)
    tabs = [full[:, :, NA_WIN_R - 1 - v:2 * NA_WIN_R - 1 - v].reshape(NA_HEADS, GRID_W, NA_WIN_R * GRID_W)
            for v in range(NA_WIN_R)]
    return jnp.stack(tabs)


def _na_attention(qkv_a, bias_tab, lc, ctx_out):
    b, t, _ = qkv_a.shape
    rows = (t - lc) // GRID_W
    tq = NA_ROWS_PER_STEP * GRID_W
    assert rows >= NA_WIN_R and rows % NA_ROWS_PER_STEP == 0 and lc % tq == 0
    ctx_steps = lc // tq if ctx_out else 0
    q_off = 0 if ctx_out else lc // tq
    n_out = t if ctx_out else t - lc
    return pl.pallas_call(
        functools.partial(_na_kernel, lc=lc, rows=rows, ctx_steps=ctx_steps),
        grid=(b, ctx_steps + rows // NA_ROWS_PER_STEP),
        in_specs=[pl.BlockSpec((None, tq, A_W), lambda bi, s: (bi, s + q_off, 0)),
                  pl.BlockSpec((None, t, A_W), lambda bi, s: (bi, 0, 1)),
                  pl.BlockSpec((None, t, A_W), lambda bi, s: (bi, 0, 2)),
                  pl.BlockSpec(bias_tab.shape, lambda bi, s: (0, 0, 0, 0))],
        out_specs=pl.BlockSpec((None, tq, A_W), lambda bi, s: (bi, s, 0)),
        out_shape=jax.ShapeDtypeStruct((b, n_out, A_W), BF16),
        compiler_params=_cparams(2), name="na_attention",
    )(qkv_a, qkv_a, qkv_a, bias_tab)


def _chunk_scans(x):
    n = x.shape[0]
    pos = lax.broadcasted_iota(jnp.int32, x.shape, 0) % DN_CHUNK
    fwd, s = x, 1
    while s < DN_CHUNK:
        fwd = fwd + jnp.where(pos >= s, pltpu.roll(fwd, s, 0), 0.0)
        s *= 2
    grouped = fwd.reshape(n // DN_CHUNK, DN_CHUNK, x.shape[1])
    total = jnp.broadcast_to(grouped[:, DN_CHUNK - 1:DN_CHUNK, :], grouped.shape).reshape(x.shape)
    return fwd, total - fwd + x


def _conv_shift_matrix():
    idx = jnp.arange(TM)
    s = jnp.zeros((2 * TM, TM + LANES), F32)
    s = s.at[idx[1:], idx[1:] - 1].set(1.0).at[0, TM + 15].set(1.0)
    s = s.at[TM + idx[:-1], idx[:-1] + 1].set(1.0).at[2 * TM - 1, TM + 16].set(1.0)
    return s.astype(BF16)


def _dn_prep_kernel(x_ref, prev_ref, next_ref, sm_ref, shift_ref, cw_ref, cos_ref, sina_ref, sinb_ref, alog_ref,
                    dtb_ref, q_ref, k_ref, v_ref, col_ref, row_ref, *, n_tiles):
    j = pl.program_id(1)
    xb = x_ref[...]
    has_prev = jnp.logical_and(j != 0, j != 1)
    has_next = jnp.logical_and(j != 0, j != n_tiles - 1)
    halo_p = jnp.where(has_prev, prev_ref[...], jnp.zeros(prev_ref.shape, prev_ref.dtype))
    halo_n = jnp.where(has_next, next_ref[...], jnp.zeros(next_ref.shape, next_ref.dtype))
    ext = jnp.concatenate([xb, halo_p, halo_n, jnp.zeros((LANES - 32, xb.shape[1]), xb.dtype)], axis=0)
    shifted = jnp.dot(shift_ref[...], ext, preferred_element_type=F32)
    xm, xp = shifted[:TM], shifted[TM:]
    y = _silu(cw_ref[0:1, :] * xm + cw_ref[1:2, :] * xb.astype(F32) + cw_ref[2:3, :] * xp)

    cos, sina, sinb = cos_ref[...], sina_ref[...], sinb_ref[...]
    for h in range(DN_HEADS):
        for base, o_ref, scale in ((0, q_ref, DN_DIM ** -0.5), (DN_W, k_ref, None)):
            cs = slice(base + h * DN_DIM, base + (h + 1) * DN_DIM)
            z = y[:, cs]
            z = z * lax.rsqrt(jnp.sum(z * z, axis=-1, keepdims=True) + EPS)
            z = z * cos + pltpu.roll(z, 3 * DN_DIM // 4, 1) * sina + pltpu.roll(z, DN_DIM // 4, 1) * sinb
            if scale is not None:
                z = z * scale
            o_ref[:, h * DN_DIM:(h + 1) * DN_DIM] = z.astype(o_ref.dtype)
    v_ref[...] = y[:, 2 * DN_W:].astype(v_ref.dtype)

    sm = sm_ref[...]
    lane = lax.broadcasted_iota(jnp.int32, sm.shape, 1)
    logg = -jnp.exp(alog_ref[...]) * jax.nn.softplus(sm + dtb_ref[...])
    gam = jnp.where(lane < 3 * DN_HEADS, *_chunk_scans(logg))
    col = jnp.where(lane < 2 * DN_HEADS, jax.nn.sigmoid(sm), gam)
    col_ref[...] = col
    row_ref[...] = col.T[0:4 * DN_HEADS, :]


def _rope_tables(lc, seq):
    t = jnp.arange(seq)
    rowp = (t // GRID_W).astype(F32)
    colp = (t % GRID_W).astype(F32)
    n_freq = DN_DIM // 4
    inv = ROPE_BASE ** (-jnp.arange(n_freq, dtype=F32) / n_freq)
    ar, ac = rowp[:, None] * inv, colp[:, None] * inv
    ang = jnp.concatenate([ar, ar, ac, ac], axis=-1)
    cos, sin = jnp.cos(ang), jnp.sin(ang)
    quarter = (jnp.arange(DN_DIM) // n_freq) % 2
    sina = jnp.where(quarter == 0, -sin, 0.0)
    sinb = jnp.where(quarter == 1, sin, 0.0)
    pad = lambda a, v: jnp.concatenate([jnp.full((lc, DN_DIM), v, F32), a], axis=0)
    return pad(cos, 1.0), pad(sina, 0.0), pad(sinb, 0.0)


def _dn_prep(qkv_b, small, conv_w, a_log, dt_bias, rope):
    b, t, _ = qkv_b.shape
    n_tiles = t // TM
    hb = TM // 16
    pad16 = lambda a: jnp.zeros((1, LANES), F32).at[0, 2 * DN_HEADS:4 * DN_HEADS].set(a.reshape(-1).astype(F32))
    outs = [(DN_W, BF16), (DN_W, BF16), (DN_W, BF16), (LANES, F32)]
    res = pl.pallas_call(
        functools.partial(_dn_prep_kernel, n_tiles=n_tiles),
        grid=(b, n_tiles),
        in_specs=[_tok_spec(3 * DN_W),
                  pl.BlockSpec((None, 16, 3 * DN_W), lambda bi, j: (bi, jnp.maximum(j * hb - 1, 0), 0)),
                  pl.BlockSpec((None, 16, 3 * DN_W), lambda bi, j: (bi, jnp.minimum((j + 1) * hb, t // 16 - 1), 0)),
                  _tok_spec(LANES),
                  _full_spec((2 * TM, TM + LANES)),
                  _full_spec((3, 3 * DN_W)),
                  pl.BlockSpec((TM, DN_DIM), lambda bi, j: (j, 0)),
                  pl.BlockSpec((TM, DN_DIM), lambda bi, j: (j, 0)),
                  pl.BlockSpec((TM, DN_DIM), lambda bi, j: (j, 0)),
                  _full_spec((1, LANES)), _full_spec((1, LANES))],
        out_specs=[_tok_spec(c) for c, _ in outs] + [pl.BlockSpec((None, 4 * DN_HEADS, TM), lambda bi, j: (bi, 0, j))],
        out_shape=[jax.ShapeDtypeStruct((b, t, c), dt) for c, dt in outs]
        + [jax.ShapeDtypeStruct((b, 4 * DN_HEADS, t), F32)],
        compiler_params=_cparams(2), name="dn_prep",
    )(qkv_b, qkv_b, qkv_b, small, _conv_shift_matrix(), conv_w.astype(F32), *rope, pad16(a_log), pad16(dt_bias))
    q, k, v, colsm, rowsm = res
    n_chunks = t // DN_CHUNK
    rowsm = rowsm.reshape(b, 4 * DN_HEADS, n_chunks, DN_CHUNK).transpose(2, 0, 1, 3)
    rowsm = rowsm.reshape(n_chunks, b, 2 * DN_HEADS, 2 * DN_CHUNK)
    return q, k, v, colsm, rowsm


def _block_diag(x):
    n, w = x.shape
    x2 = jnp.concatenate([x, x], axis=0)
    row = lax.broadcasted_iota(jnp.int32, x2.shape, 0)
    lane = lax.broadcasted_iota(jnp.int32, x2.shape, 1)
    return jnp.where((row // n) == (lane // (w // 2)), x2, jnp.zeros_like(x2))


def _dn_pair_step(d, p, q_all, k_all, v_all, cm, rm, s_prev):
    c = DN_CHUNK
    heads = (2 * p, 2 * p + 1)
    cs2 = slice(2 * p * DN_DIM, (2 * p + 2) * DN_DIM)
    q2, k2, v2 = q_all[:, cs2], k_all[:, cs2], v_all[:, cs2]
    lane = lax.broadcasted_iota(jnp.int32, (c, 2 * c), 1)
    ri = lax.broadcasted_iota(jnp.int32, (c, 2 * c), 0)
    ci = lane % c
    first = lane < c
    col = lambda base: [cm[:, base + h:base + h + 1] for h in heads]
    beta, gam_c = col(d * DN_HEADS), col(2 * DN_HEADS + d * DN_HEADS)
    side = lambda ab: jnp.where(first, ab[0], ab[1])
    ir = (2 * DN_HEADS + d * DN_HEADS) // 2 + p
    gam_r = rm[ir:ir + 1, :]
    if d == 0:
        incl, strict, g_last = ri >= ci, ri > ci, [g[c - 1:c, :] for g in gam_c]
    else:
        incl, strict, g_last = ri <= ci, ri < ci, [g[0:1, :] for g in gam_c]
    decay = jnp.exp(jnp.where(incl, side(gam_c) - gam_r, -jnp.inf))
    zk = jnp.zeros((c, DN_DIM), BF16)
    kd = jnp.concatenate([jnp.concatenate([k2[:, :DN_DIM], zk], axis=1),
                          jnp.concatenate([zk, k2[:, DN_DIM:]], axis=1)], axis=0)
    g = lax.dot_general(jnp.concatenate([k2, q2], axis=0), kd, (((1,), (1,)), ((), ())),
                        preferred_element_type=F32)
    yield
    kk, qk = g[:c], g[c:]
    lneg = jnp.where(strict, -(side(beta) * kk * decay), 0.0)
    aqk = qk * decay
    eg = [jnp.exp(gc) for gc in gam_c]
    kf = [k2[:, i * DN_DIM:(i + 1) * DN_DIM].astype(F32) for i in range(2)]
    vf = [v2[:, i * DN_DIM:(i + 1) * DN_DIM].astype(F32) for i in range(2)]
    qf = [q2[:, i * DN_DIM:(i + 1) * DN_DIM].astype(F32) for i in range(2)]
    rhs = [jnp.concatenate([(beta[i] * eg[i]) * kf[i], beta[i] * vf[i]], axis=1).astype(BF16) for i in range(2)]
    eye = (ri == ci).astype(F32)
    same_blk = lambda s: (ri // s) == (ci // s)
    dneg = jnp.where(same_blk(4), lneg, 0.0)
    dnb = dneg.astype(BF16)
    dsq = jnp.dot(dnb, _block_diag(dnb), preferred_element_type=F32)
    yield
    t = (eye + dneg) + jnp.dot((eye + dneg).astype(BF16), _block_diag(dsq.astype(BF16)), preferred_element_type=F32)
    yield
    s = 4
    while s < c:
        eneg = jnp.where(jnp.logical_and(same_blk(2 * s), jnp.logical_not(same_blk(s))), lneg, 0.0)
        tb = t.astype(BF16)
        te = jnp.dot(tb, _block_diag(eneg.astype(BF16)), preferred_element_type=F32)
        yield
        t = t + jnp.dot(te.astype(BF16), _block_diag(tb), preferred_element_type=F32)
        yield
        s *= 2
    sol = jnp.dot(_block_diag(t.astype(BF16)), jnp.concatenate(rhs, axis=0), preferred_element_type=F32).astype(BF16)
    yield
    kdec = jnp.concatenate([kf[i] * jnp.exp(g_last[i] - gam_c[i]) for i in range(2)], axis=0)
    lhs = jnp.concatenate([_block_diag(aqk.astype(BF16)), _block_diag(kdec.T.astype(BF16))], axis=0)
    res = jnp.dot(lhs, sol, preferred_element_type=F32)
    yield
    outs, states = [], []
    for i in range(2):
        qo = res[i * c:(i + 1) * c]
        ab = res[2 * c + i * DN_DIM:2 * c + (i + 1) * DN_DIM]
        qeff = qf[i] * eg[i] - qo[:, :DN_DIM]
        both = jnp.dot(jnp.concatenate([qeff, ab[:, :DN_DIM]], axis=0).astype(BF16), s_prev[i].astype(BF16),
                       preferred_element_type=F32)
        outs.append(both[:c] + qo[:, DN_DIM:])
        states.append(jnp.exp(g_last[i]) * s_prev[i] - both[c:] + ab[:, DN_DIM:])
    return outs, states


def _dn_chain_kernel(qf_ref, kf_ref, vf_ref, cf_ref, rf_ref, qb_ref, kb_ref, vb_ref, cb_ref, rb_ref,
                     of_ref, ob_ref, s_ref, *, nb):
    @pl.when(pl.program_id(0) == 0)
    def _():
        s_ref[...] = jnp.zeros_like(s_ref)

    in_refs = ((qf_ref, kf_ref, vf_ref, cf_ref, rf_ref), (qb_ref, kb_ref, vb_ref, cb_ref, rb_ref))
    per_iter = DN_BATCH_UNROLL if nb % DN_BATCH_UNROLL == 0 else 1

    def body(i, carry):
        bs = [i * per_iter + u for u in range(per_iter)]
        n_pairs = DN_HEADS // 2
        gens = [_dn_pair_step(d, p, *[r[b] for r in in_refs[d]], [s_ref[d, b, 2 * p], s_ref[d, b, 2 * p + 1]])
                for b in bs for d in range(2) for p in range(n_pairs)]
        res = _lockstep(gens)
        for ib, b in enumerate(bs):
            for d, o_r in enumerate((of_ref, ob_ref)):
                base = (ib * 2 + d) * n_pairs
                o_r[b] = jnp.concatenate([o for p in range(n_pairs) for o in res[base + p][0]], axis=1)
                for p in range(n_pairs):
                    for i in range(2):
                        s_ref[d, b, 2 * p + i] = res[base + p][1][i]
        return carry

    lax.fori_loop(0, nb // per_iter, body, 0)


def _dn_chain(q, k, v, colsm, rowsm, lc):
    b, t, _ = q.shape
    n_chunks = t // DN_CHUNK
    nc_ctx = lc // DN_CHUNK
    fwd = lambda i: i
    bwd = lambda i: jnp.where(i < nc_ctx, nc_ctx - 1 - i, n_chunks - 1 + nc_ctx - i)
    tok = lambda cols, cm: pl.BlockSpec((b, DN_CHUNK, cols), lambda i: (0, cm(i), 0))
    rowspec = lambda cm: pl.BlockSpec((None, b, 2 * DN_HEADS, 2 * DN_CHUNK), lambda i: (cm(i), 0, 0, 0))
    in_specs = []
    for cm in (fwd, bwd):
        in_specs += [tok(DN_W, cm), tok(DN_W, cm), tok(DN_W, cm), tok(LANES, cm), rowspec(cm)]
    return pl.pallas_call(
        functools.partial(_dn_chain_kernel, nb=b),
        grid=(n_chunks,),
        in_specs=in_specs,
        out_specs=[tok(DN_W, fwd), tok(DN_W, bwd)],
        out_shape=[jax.ShapeDtypeStruct((b, t, DN_W), F32)] * 2,
        scratch_shapes=[pltpu.VMEM((2, b, DN_HEADS, DN_DIM, DN_DIM), F32)],
        compiler_params=pltpu.CompilerParams(dimension_semantics=("arbitrary",), vmem_limit_bytes=VMEM_LIMIT),
        name="dn_chain",
    )(q, k, v, colsm, rowsm, q, k, v, colsm, rowsm)


def _ffn_math(x, g, sc, sh, w1_ref, w2_ref, gt, gf, final_norm):
    h = _norm_mod(x, g, sc, sh).astype(BF16)
    a = jnp.maximum(jnp.dot(h, w1_ref[...], preferred_element_type=F32), 0.0)
    y = jnp.dot((a * a).astype(BF16), w2_ref[...], preferred_element_type=F32)
    out = x + gt * y
    if final_norm:
        out = out * lax.rsqrt(jnp.mean(out * out, axis=-1, keepdims=True) + EPS) * gf
    return out


def _even_out_ffn_kernel(x_ref, att_ref, of_ref, ob_ref, gate_ref, gout_ref, w_ref, gt1_ref,
                         g_ref, sc_ref, sh_ref, w1_ref, w2_ref, gt2_ref, gf_ref, o_ref, *, final_norm):
    o = of_ref[...] + ob_ref[...]
    gate = gate_ref[...].astype(F32)
    parts = [att_ref[...]]
    for h in range(DN_HEADS):
        cs = slice(h * DN_DIM, (h + 1) * DN_DIM)
        oh = o[:, cs]
        yh = oh * lax.rsqrt(jnp.mean(oh * oh, axis=-1, keepdims=True) + EPS) * gout_ref[...]
        parts.append((yh * _silu(gate[:, cs])).astype(BF16))
    a = jnp.concatenate(parts, axis=1)
    y = jnp.dot(a, w_ref[...], preferred_element_type=F32)
    x1 = x_ref[...] + gt1_ref[...] * y
    o_ref[...] = _ffn_math(x1, g_ref[...], sc_ref[...], sh_ref[...], w1_ref, w2_ref, gt2_ref[...], gf_ref[...],
                           final_norm)


def _even_out_ffn(x, att, o_f, o_b, gate, g_out, w_out, gt1, g2, sc2, sh2, w1, w2, gt2, g_final, ctx_out, final_norm):
    b, t, d = x.shape
    off = 0 if ctx_out else 1
    n_t = t // TM - off
    mod = lambda: _mod_spec(d, ctx_out)
    return pl.pallas_call(
        functools.partial(_even_out_ffn_kernel, final_norm=final_norm),
        grid=(b, n_t),
        in_specs=[_tok_spec(d, off), _tok_spec(A_W), _tok_spec(DN_W, off), _tok_spec(DN_W, off), _tok_spec(DN_W, off),
                  _full_spec((1, DN_DIM)), _full_spec((A_W + DN_W, d)), mod(),
                  _full_spec((1, d)), mod(), mod(), _full_spec(w1.shape), _full_spec(w2.shape), mod(),
                  _full_spec((1, d))],
        out_specs=_tok_spec(d),
        out_shape=jax.ShapeDtypeStruct((b, n_t * TM, d), F32),
        compiler_params=_cparams(2), name="even_out_ffn",
    )(x, att, o_f, o_b, gate, g_out, w_out, gt1, g2, sc2, sh2, w1, w2, gt2, g_final)


def _gelu(x):
    return 0.5 * x * (1.0 + lax.erf(x * (2.0 ** -0.5)))


def _skewed(gens):
    results = [None] * len(gens)
    live, started = [], 0
    while started < len(gens) or live:
        if started < len(gens):
            live.append(started)
            started += 1
        still = []
        for i in reversed(live):
            try:
                next(gens[i])
                still.append(i)
            except StopIteration as stop:
                results[i] = stop.value
        live = still[::-1]
    return results


def _gmlp_kernel(x_ref, g_ref, sc_ref, sh_ref, wi_ref, gv_ref, ws_ref, bs_ref, wo_ref, gt_ref, o_ref, u_ref, v_ref):
    x = x_ref[...]
    h = _norm_mod(x, g_ref[...], sc_ref[...], sh_ref[...]).astype(BF16)
    half = wi_ref.shape[1] // 2
    gw = half // GM_GROUPS

    pw = 2 * gw
    n_pairs = GM_GROUPS // 2

    def project(c0, dst_ref, want_ssq):
        z = jnp.dot(h, wi_ref[:, c0:c0 + pw], preferred_element_type=F32)
        yield
        z = _gelu(z)
        d0 = c0 % half
        dst_ref[:, d0:d0 + pw] = z.astype(dst_ref.dtype)
        return jnp.sum(z * z, axis=-1, keepdims=True) if want_ssq else None

    cols = [(half + p * pw, v_ref, True) for p in range(n_pairs)] + [(p * pw, u_ref, False) for p in range(n_pairs)]
    ssq = functools.reduce(jnp.add, _skewed([project(*c) for c in cols])[:n_pairs])
    rinv = lax.rsqrt(ssq * (1.0 / half) + EPS)

    def mix(p):
        cs = slice(p * pw, (p + 1) * pw)
        vb = (v_ref[:, cs] * rinv * gv_ref[:, cs]).astype(BF16)
        mixed = [[jnp.dot(ws_ref[2 * p + k], vb[n * GM_CHUNK:(n + 1) * GM_CHUNK, k * gw:(k + 1) * gw],
                          preferred_element_type=F32) for k in range(2)] for n in range(TM // GM_CHUNK)]
        yield
        bias = [jnp.concatenate([bs_ref[2 * p + k]] * (gw // LANES), axis=1) for k in range(2)]
        t = jnp.concatenate([jnp.concatenate([row[k] + bias[k] for k in range(2)], axis=1) for row in mixed], axis=0)
        t = (t * u_ref[:, cs]).astype(BF16)
        return jnp.dot(t, wo_ref[cs, :], preferred_element_type=F32)

    y = functools.reduce(jnp.add, _skewed([mix(p) for p in range(n_pairs)]))
    o_ref[...] = x + gt_ref[...] * y


def _gmlp(x, g, sc, sh, w_in, g_v, ws, bs, w_out, gt, has_ctx):
    b, t, d = x.shape
    half = w_in.shape[1] // 2
    assert (half // GM_GROUPS) % LANES == 0
    return pl.pallas_call(
        _gmlp_kernel,
        grid=(b, t // TM),
        in_specs=[_tok_spec(d), _full_spec((1, d)), _mod_spec(d, has_ctx), _mod_spec(d, has_ctx),
                  _full_spec(w_in.shape), _full_spec((1, half)), _full_spec(ws.shape), _full_spec(bs.shape),
                  _full_spec(w_out.shape), _mod_spec(d, has_ctx)],
        out_specs=_tok_spec(d),
        out_shape=jax.ShapeDtypeStruct((b, t, d), F32),
        scratch_shapes=[pltpu.VMEM((TM, half), F32), pltpu.VMEM((TM, half), F32)],
        compiler_params=_cparams(2), name="gmlp",
    )(x, g, sc, sh, w_in, g_v, ws, bs, w_out, gt)


def _ffn_kernel(x_ref, g_ref, sc_ref, sh_ref, w1_ref, w2_ref, gt_ref, gf_ref, o_ref, *, final_norm):
    o_ref[...] = _ffn_math(x_ref[...], g_ref[...], sc_ref[...], sh_ref[...], w1_ref, w2_ref, gt_ref[...], gf_ref[...],
                           final_norm)


def _ffn(x, g, sc, sh, w1, w2, gt, g_final, has_ctx, final_norm):
    b, t, d = x.shape
    return pl.pallas_call(
        functools.partial(_ffn_kernel, final_norm=final_norm),
        grid=(b, t // TM),
        in_specs=[_tok_spec(d), _full_spec((1, d)), _mod_spec(d, has_ctx), _mod_spec(d, has_ctx),
                  _full_spec(w1.shape), _full_spec(w2.shape), _mod_spec(d, has_ctx), _full_spec((1, d))],
        out_specs=_tok_spec(d),
        out_shape=jax.ShapeDtypeStruct((b, t, d), F32),
        compiler_params=_cparams(2), name="ffn",
    )(x, g, sc, sh, w1, w2, gt, g_final)


def kernel(x, c, ctx, c_ctx, w_ada, b_ada, g_norm_mix, g_norm_ffn, w_in_even, w_out_even, na_rpb, dn_conv, dn_a_log,
           dn_dt_bias, dn_g_out, w_in_odd, gm_g_v, gm_ws, gm_bs, w_out_odd, w_ff1, w_ff2, g_final):
    nb, seq, d = x.shape
    lc = ctx.shape[1]
    depth = w_ada.shape[0]
    assert nb <= 8 and seq % TM == 0 and lc == TM and seq % GRID_W == 0

    cond = jnp.zeros((16, d), F32).at[:nb].set(c).at[8].set(c_ctx)
    mods = _adaln(cond, w_ada, b_ada).reshape(depth, 16, 6, 1, d)
    rope = _rope_tables(lc, seq)
    row = lambda a: a.reshape(1, -1).astype(F32)

    xs = jnp.concatenate([ctx, x], axis=1)
    has_ctx = True
    for l in range(depth):
        ctx_live = any(j % 2 == 0 for j in range(l + 1, depth))
        sh1, sc1, gt1, sh2, sc2, gt2 = [mods[l, :, i] for i in range(6)]
        if l % 2 == 0:
            assert has_ctx
            e = l // 2
            w_pad = jnp.pad(w_in_even[e], ((0, 0), (0, LANES - 4 * DN_HEADS))).astype(BF16)
            qkv_a, qkv_b, gate, small = _even_in(xs, row(g_norm_mix[l]), sc1, sh1, w_pad)
            att = _na_attention(qkv_a, _na_bias_table(na_rpb[e]), lc, ctx_live)
            q, k, v, colsm, rowsm = _dn_prep(qkv_b, small, dn_conv[e], dn_a_log[e], dn_dt_bias[e], rope)
            o_f, o_b = _dn_chain(q, k, v, colsm, rowsm, lc)
            xs = _even_out_ffn(xs, att, o_f, o_b, gate, row(dn_g_out[e]), w_out_even[e].astype(BF16), gt1,
                               row(g_norm_ffn[l]), sc2, sh2, w_ff1[l].astype(BF16), w_ff2[l].astype(BF16), gt2,
                               row(g_final), ctx_live, l == depth - 1)
            has_ctx = ctx_live
        else:
            o = l // 2
            bs = jnp.broadcast_to(gm_bs[o].astype(F32)[:, :, None], (GM_GROUPS, GM_CHUNK, LANES))
            xs = _gmlp(xs, row(g_norm_mix[l]), sc1, sh1, w_in_odd[o].astype(BF16), row(gm_g_v[o]),
                       gm_ws[o].astype(BF16), bs, w_out_odd[o].astype(BF16), gt1, has_ctx)
            xs = _ffn(xs, row(g_norm_ffn[l]), sc2, sh2, w_ff1[l].astype(BF16), w_ff2[l].astype(BF16), gt2,
                      row(g_final), has_ctx, l == depth - 1)
    return xs[:, lc:] if has_ctx else xs
```

```python
import functools

import jax
import jax.numpy as jnp
from jax import lax
from jax.experimental import pallas as pl
from jax.experimental.pallas import tpu as pltpu

EPS = 1e-6
GRID_W = 64
NA_HEADS, NA_DIM, NA_WIN_R, NA_WIN_C = 8, 64, 8, 16
A_W = NA_HEADS * NA_DIM
DN_HEADS, DN_DIM, DN_CHUNK = 4, 128, 64
DN_W = DN_HEADS * DN_DIM
ROPE_BASE = 10000.0
GM_GROUPS, GM_CHUNK = 8, 128

LANES = 128
TM = 256
DN_BATCH_UNROLL = 4
NA_ROWS_PER_STEP = 4
NA_CHAINS_IN_FLIGHT = 8
VMEM_LIMIT = 56 * 1024 * 1024
F32, BF16 = jnp.float32, jnp.bfloat16


def _cparams(n_axes):
    return pltpu.CompilerParams(dimension_semantics=("parallel",) * n_axes, vmem_limit_bytes=VMEM_LIMIT)


def _tok_spec(cols, off=0, col_block=0):
    return pl.BlockSpec((None, TM, cols), lambda b, j: (b, j + off, col_block))


def _mod_spec(d, ctx_tile):
    if ctx_tile:
        return pl.BlockSpec((None, 1, d), lambda b, j: (jnp.where(j == 0, 8, b), 0, 0))
    return pl.BlockSpec((None, 1, d), lambda b, j: (b, 0, 0))


def _full_spec(shape):
    n = len(shape)
    return pl.BlockSpec(shape, lambda b, j: (0,) * n)


def _layer_spec(w, l):
    return pl.BlockSpec((None,) + w.shape[1:], lambda b, j: (l,) + (0,) * (w.ndim - 1))


def _stream_specs(xs, d, off=0):
    if isinstance(xs, tuple):
        return [pl.BlockSpec((None, TM, d), lambda b, j: (b, 0, 0)),
                pl.BlockSpec((None, TM, d), lambda b, j: (b, jnp.maximum(j - 1, 0), 0))]
    return [_tok_spec(d, off)]


def _stream_tile(refs):
    if len(refs) == 2:
        return jnp.where(pl.program_id(1) == 0, refs[0][...], refs[1][...])
    return refs[0][...]


def _stream_args(xs):
    return list(xs) if isinstance(xs, tuple) else [xs]


def _stream_shape(xs):
    if isinstance(xs, tuple):
        return xs[1].shape[0], xs[0].shape[1] + xs[1].shape[1], xs[1].shape[2]
    return xs.shape


def _norm_mod(x, g, sc, sh):
    y = x * lax.rsqrt(jnp.mean(x * x, axis=-1, keepdims=True) + EPS)
    return (y * g) * (1.0 + sc) + sh


def _silu(x):
    return x * jax.nn.sigmoid(x)


def _adaln_kernel(cond_ref, w_ref, b_ref, o_ref):
    a = _silu(cond_ref[...]).astype(BF16)
    o_ref[...] = jnp.dot(a, w_ref[...].astype(BF16), preferred_element_type=F32) + b_ref[...]


def _adaln(cond, w_ada, b_ada):
    depth, d, n = w_ada.shape
    tn = n // 4
    return pl.pallas_call(
        _adaln_kernel,
        grid=(depth, n // tn),
        in_specs=[pl.BlockSpec((16, d), lambda l, j: (0, 0)),
                  pl.BlockSpec((None, d, tn), lambda l, j: (l, 0, j)),
                  pl.BlockSpec((None, 1, tn), lambda l, j: (l, 0, j))],
        out_specs=pl.BlockSpec((None, 16, tn), lambda l, j: (l, 0, j)),
        out_shape=jax.ShapeDtypeStruct((depth, 16, n), F32),
        compiler_params=_cparams(2), name="adaln",
    )(cond, w_ada, b_ada.reshape(depth, 1, n))


def _even_in_kernel(*refs, n_x):
    g_ref, sc_ref, sh_ref, w_ref, qa_ref, qb_ref, gate_ref, sm_ref = refs[n_x:]
    h = _norm_mod(_stream_tile(refs[:n_x]), g_ref[...], sc_ref[...], sh_ref[...]).astype(BF16)
    p = jnp.dot(h, w_ref[...], preferred_element_type=F32)
    qa_ref[...] = p[:, :3 * A_W].astype(BF16)
    qb_ref[...] = p[:, 3 * A_W:3 * A_W + 3 * DN_W].astype(BF16)
    gate_ref[...] = p[:, 3 * A_W + 3 * DN_W:3 * A_W + 4 * DN_W].astype(BF16)
    sm_ref[...] = p[:, 3 * A_W + 4 * DN_W:]


def _even_in(xs, g, sc, sh, w_pad, e):
    b, t, d = _stream_shape(xs)
    outs = [(3 * A_W, BF16), (3 * DN_W, BF16), (DN_W, BF16), (LANES, F32)]
    x_args = _stream_args(xs)
    return pl.pallas_call(
        functools.partial(_even_in_kernel, n_x=len(x_args)),
        grid=(b, t // TM),
        in_specs=_stream_specs(xs, d) + [_full_spec((1, d)), _mod_spec(d, True), _mod_spec(d, True),
                                         _layer_spec(w_pad, e)],
        out_specs=[_tok_spec(c) for c, _ in outs],
        out_shape=[jax.ShapeDtypeStruct((b, t, c), dt) for c, dt in outs],
        compiler_params=_cparams(2), name="even_in",
    )(*x_args, g, sc, sh, w_pad)


def _lockstep(gens):
    results = [None] * len(gens)
    live = list(range(len(gens)))
    while live:
        still = []
        for i in live:
            try:
                next(gens[i])
                still.append(i)
            except StopIteration as stop:
                results[i] = stop.value
        live = still
    return results


def _softmax_head_pair(q, keys, vals, biases):
    wq = q.shape[0]
    lane = lax.broadcasted_iota(jnp.int32, (2 * wq, LANES), 1)
    row = lax.broadcasted_iota(jnp.int32, (2 * wq, LANES), 0)
    own = (lane // NA_DIM) == (row // wq)
    q2 = jnp.concatenate([q, q], axis=0)
    qm = jnp.where(own, q2, jnp.zeros_like(q2))
    ss = [lax.dot_general(qm, kb, (((1,), (1,)), ((), ())), preferred_element_type=F32) for kb in keys]
    yield
    ss = [s if bias is None else s + bias for s, bias in zip(ss, biases)]
    m = functools.reduce(jnp.maximum, [jnp.max(s, axis=-1, keepdims=True) for s in ss])
    yield
    ps = [jnp.exp(s - m) for s in ss]
    den = functools.reduce(jnp.add, [jnp.sum(p, axis=-1, keepdims=True) for p in ps])
    yield
    o = functools.reduce(jnp.add, [jnp.dot(p.astype(BF16), vb, preferred_element_type=F32)
                                   for p, vb in zip(ps, vals)])
    yield
    o = o / den
    return jnp.where(lax.broadcasted_iota(jnp.int32, (wq, LANES), 1) < NA_DIM, o[:wq], o[wq:])


def _attend_jobs(q_ref, o_ref, jobs):
    scale = NA_DIM ** -0.5
    gens, dests = [], []
    for rsl, blocks, bias_of in jobs:
        for pair in range(NA_HEADS // 2):
            cs = slice(pair * LANES, (pair + 1) * LANES)
            keys, vals = blocks(cs)
            gens.append(_softmax_head_pair(q_ref[rsl, cs] * scale, keys, vals, bias_of(pair)))
            dests.append((rsl, cs))
    for g0 in range(0, len(gens), NA_CHAINS_IN_FLIGHT):
        outs = _lockstep(gens[g0:g0 + NA_CHAINS_IN_FLIGHT])
        for (rsl, cs), o in zip(dests[g0:g0 + NA_CHAINS_IN_FLIGHT], outs):
            o_ref[rsl, cs] = o.astype(o_ref.dtype)


def _na_kernel(q_ref, k_ref, v_ref, bias_ref, o_ref, *, lc, rows, ctx_steps):
    step = pl.program_id(1)
    n_loc = NA_WIN_R * GRID_W

    def latent_rows():
        jobs = []
        for i in range(NA_ROWS_PER_STEP):
            r = (step - ctx_steps) * NA_ROWS_PER_STEP + i
            rs = jnp.clip(r - NA_WIN_R // 2, 0, rows - NA_WIN_R)
            start = pl.multiple_of(lc + rs * GRID_W, GRID_W)
            blocks = lambda cs, start=start: ([k_ref[pl.ds(start, n_loc), cs], k_ref[0:lc, cs]],
                                              [v_ref[pl.ds(start, n_loc), cs], v_ref[0:lc, cs]])
            bias_of = lambda pair, var=r - rs: [bias_ref[var, 2 * pair:2 * pair + 2].reshape(2 * GRID_W, n_loc), None]
            jobs.append((slice(i * GRID_W, (i + 1) * GRID_W), blocks, bias_of))
        _attend_jobs(q_ref, o_ref, jobs)

    def context_block():
        blocks = lambda cs: ([k_ref[0:lc, cs]], [v_ref[0:lc, cs]])
        _attend_jobs(q_ref, o_ref, [(slice(None), blocks, lambda pair: [None])])

    if ctx_steps:
        pl.when(step < ctx_steps)(context_block)
        pl.when(step >= ctx_steps)(latent_rows)
    else:
        latent_rows()


def _na_bias_table(rpb):
    col = jnp.arange(GRID_W)
    cstart = jnp.clip(col - NA_WIN_C // 2, 0, GRID_W - NA_WIN_C)
    col_ok = (col[None, :] >= cstart[:, None]) & (col[None, :] < cstart[:, None] + NA_WIN_C)
    dc_idx = jnp.clip(col[None, :] - col[:, None], -(NA_WIN_C - 1), NA_WIN_C - 1) + NA_WIN_C - 1
    onehot = (dc_idx[:, :, None] == jnp.arange(2 * NA_WIN_C - 1)).astype(F32)
    full = jnp.einsum('hij,qkj->hqik', rpb.astype(F32), onehot, precision=lax.Precision.HIGHEST)
    full = jnp.where(col_ok[None, :, None, :], full, -jnp.inf)
    tabs = [full[:, :, NA_WIN_R - 1 - v:2 * NA_WIN_R - 1 - v].reshape(NA_HEADS, GRID_W, NA_WIN_R * GRID_W)
            for v in range(NA_WIN_R)]
    return jnp.stack(tabs)


def _na_attention(qkv_a, bias_tab, lc, ctx_out):
    b, t, _ = qkv_a.shape
    rows = (t - lc) // GRID_W
    tq = NA_ROWS_PER_STEP * GRID_W
    assert rows >= NA_WIN_R and rows % NA_ROWS_PER_STEP == 0 and lc % tq == 0
    ctx_steps = lc // tq if ctx_out else 0
    q_off = 0 if ctx_out else lc // tq
    n_out = t if ctx_out else t - lc
    return pl.pallas_call(
        functools.partial(_na_kernel, lc=lc, rows=rows, ctx_steps=ctx_steps),
        grid=(b, ctx_steps + rows // NA_ROWS_PER_STEP),
        in_specs=[pl.BlockSpec((None, tq, A_W), lambda bi, s: (bi, s + q_off, 0)),
                  pl.BlockSpec((None, t, A_W), lambda bi, s: (bi, 0, 1)),
                  pl.BlockSpec((None, t, A_W), lambda bi, s: (bi, 0, 2)),
                  pl.BlockSpec(bias_tab.shape, lambda bi, s: (0, 0, 0, 0))],
        out_specs=pl.BlockSpec((None, tq, A_W), lambda bi, s: (bi, s, 0)),
        out_shape=jax.ShapeDtypeStruct((b, n_out, A_W), BF16),
        compiler_params=_cparams(2), name="na_attention",
    )(qkv_a, qkv_a, qkv_a, bias_tab)


def _chunk_scans(x):
    n = x.shape[0]
    pos = lax.broadcasted_iota(jnp.int32, x.shape, 0) % DN_CHUNK
    fwd, s = x, 1
    while s < DN_CHUNK:
        fwd = fwd + jnp.where(pos >= s, pltpu.roll(fwd, s, 0), 0.0)
        s *= 2
    grouped = fwd.reshape(n // DN_CHUNK, DN_CHUNK, x.shape[1])
    total = jnp.broadcast_to(grouped[:, DN_CHUNK - 1:DN_CHUNK, :], grouped.shape).reshape(x.shape)
    return fwd, total - fwd + x


def _conv_shift_matrix():
    idx = jnp.arange(TM)
    s = jnp.zeros((2 * TM, TM + LANES), F32)
    s = s.at[idx[1:], idx[1:] - 1].set(1.0).at[0, TM + 15].set(1.0)
    s = s.at[TM + idx[:-1], idx[:-1] + 1].set(1.0).at[2 * TM - 1, TM + 16].set(1.0)
    return s.astype(BF16)


def _dn_prep_kernel(x_ref, prev_ref, next_ref, sm_ref, shift_ref, cw_ref, cos_ref, sina_ref, sinb_ref, alog_ref,
                    dtb_ref, q_ref, k_ref, v_ref, col_ref, row_ref, *, n_tiles):
    j = pl.program_id(1)
    xb = x_ref[...]
    has_prev = jnp.logical_and(j != 0, j != 1)
    has_next = jnp.logical_and(j != 0, j != n_tiles - 1)
    halo_p = jnp.where(has_prev, prev_ref[...], jnp.zeros(prev_ref.shape, prev_ref.dtype))
    halo_n = jnp.where(has_next, next_ref[...], jnp.zeros(next_ref.shape, next_ref.dtype))
    ext = jnp.concatenate([xb, halo_p, halo_n, jnp.zeros((LANES - 32, xb.shape[1]), xb.dtype)], axis=0)
    shifted = jnp.dot(shift_ref[...], ext, preferred_element_type=F32)
    xm, xp = shifted[:TM], shifted[TM:]
    y = _silu(cw_ref[0:1, :] * xm + cw_ref[1:2, :] * xb.astype(F32) + cw_ref[2:3, :] * xp)

    cos, sina, sinb = cos_ref[...], sina_ref[...], sinb_ref[...]
    for h in range(DN_HEADS):
        for base, o_ref, scale in ((0, q_ref, DN_DIM ** -0.5), (DN_W, k_ref, None)):
            cs = slice(base + h * DN_DIM, base + (h + 1) * DN_DIM)
            z = y[:, cs]
            z = z * lax.rsqrt(jnp.sum(z * z, axis=-1, keepdims=True) + EPS)
            z = z * cos + pltpu.roll(z, 3 * DN_DIM // 4, 1) * sina + pltpu.roll(z, DN_DIM // 4, 1) * sinb
            if scale is not None:
                z = z * scale
            o_ref[:, h * DN_DIM:(h + 1) * DN_DIM] = z.astype(o_ref.dtype)
    v_ref[...] = y[:, 2 * DN_W:].astype(v_ref.dtype)

    sm = sm_ref[...]
    lane = lax.broadcasted_iota(jnp.int32, sm.shape, 1)
    logg = -jnp.exp(alog_ref[...]) * jax.nn.softplus(sm + dtb_ref[...])
    gam = jnp.where(lane < 3 * DN_HEADS, *_chunk_scans(logg))
    col = jnp.where(lane < 2 * DN_HEADS, jax.nn.sigmoid(sm), gam)
    col_ref[...] = col
    row_ref[...] = col.T[0:4 * DN_HEADS, :]


def _rope_tables(lc, seq):
    t = jnp.arange(seq)
    rowp = (t // GRID_W).astype(F32)
    colp = (t % GRID_W).astype(F32)
    n_freq = DN_DIM // 4
    inv = ROPE_BASE ** (-jnp.arange(n_freq, dtype=F32) / n_freq)
    ar, ac = rowp[:, None] * inv, colp[:, None] * inv
    ang = jnp.concatenate([ar, ar, ac, ac], axis=-1)
    cos, sin = jnp.cos(ang), jnp.sin(ang)
    quarter = (jnp.arange(DN_DIM) // n_freq) % 2
    sina = jnp.where(quarter == 0, -sin, 0.0)
    sinb = jnp.where(quarter == 1, sin, 0.0)
    pad = lambda a, v: jnp.concatenate([jnp.full((lc, DN_DIM), v, F32), a], axis=0)
    return pad(cos, 1.0), pad(sina, 0.0), pad(sinb, 0.0)


def _dn_prep(qkv_b, small, conv_w, a_log, dt_bias, rope):
    b, t, _ = qkv_b.shape
    n_tiles = t // TM
    hb = TM // 16
    pad16 = lambda a: jnp.zeros((1, LANES), F32).at[0, 2 * DN_HEADS:4 * DN_HEADS].set(a.reshape(-1).astype(F32))
    outs = [(DN_W, BF16), (DN_W, BF16), (DN_W, BF16), (LANES, F32)]
    res = pl.pallas_call(
        functools.partial(_dn_prep_kernel, n_tiles=n_tiles),
        grid=(b, n_tiles),
        in_specs=[_tok_spec(3 * DN_W),
                  pl.BlockSpec((None, 16, 3 * DN_W), lambda bi, j: (bi, jnp.maximum(j * hb - 1, 0), 0)),
                  pl.BlockSpec((None, 16, 3 * DN_W), lambda bi, j: (bi, jnp.minimum((j + 1) * hb, t // 16 - 1), 0)),
                  _tok_spec(LANES),
                  _full_spec((2 * TM, TM + LANES)),
                  _full_spec((3, 3 * DN_W)),
                  pl.BlockSpec((TM, DN_DIM), lambda bi, j: (j, 0)),
                  pl.BlockSpec((TM, DN_DIM), lambda bi, j: (j, 0)),
                  pl.BlockSpec((TM, DN_DIM), lambda bi, j: (j, 0)),
                  _full_spec((1, LANES)), _full_spec((1, LANES))],
        out_specs=[_tok_spec(c) for c, _ in outs] + [pl.BlockSpec((None, 4 * DN_HEADS, TM), lambda bi, j: (bi, 0, j))],
        out_shape=[jax.ShapeDtypeStruct((b, t, c), dt) for c, dt in outs]
        + [jax.ShapeDtypeStruct((b, 4 * DN_HEADS, t), F32)],
        compiler_params=_cparams(2), name="dn_prep",
    )(qkv_b, qkv_b, qkv_b, small, _conv_shift_matrix(), conv_w.astype(F32), *rope, pad16(a_log), pad16(dt_bias))
    q, k, v, colsm, rowsm = res
    n_chunks = t // DN_CHUNK
    rowsm = rowsm.reshape(b, 4 * DN_HEADS, n_chunks, DN_CHUNK).transpose(2, 0, 1, 3)
    rowsm = rowsm.reshape(n_chunks, b, 2 * DN_HEADS, 2 * DN_CHUNK)
    return q, k, v, colsm, rowsm


def _block_diag(x):
    n, w = x.shape
    x2 = jnp.concatenate([x, x], axis=0)
    row = lax.broadcasted_iota(jnp.int32, x2.shape, 0)
    lane = lax.broadcasted_iota(jnp.int32, x2.shape, 1)
    return jnp.where((row // n) == (lane // (w // 2)), x2, jnp.zeros_like(x2))


def _dn_pair_step(d, p, q_all, k_all, v_all, cm, rm, s_prev):
    c = DN_CHUNK
    heads = (2 * p, 2 * p + 1)
    cs2 = slice(2 * p * DN_DIM, (2 * p + 2) * DN_DIM)
    q2, k2, v2 = q_all[:, cs2], k_all[:, cs2], v_all[:, cs2]
    lane = lax.broadcasted_iota(jnp.int32, (c, 2 * c), 1)
    ri = lax.broadcasted_iota(jnp.int32, (c, 2 * c), 0)
    ci = lane % c
    first = lane < c
    col = lambda base: [cm[:, base + h:base + h + 1] for h in heads]
    beta, gam_c = col(d * DN_HEADS), col(2 * DN_HEADS + d * DN_HEADS)
    side = lambda ab: jnp.where(first, ab[0], ab[1])
    ir = (2 * DN_HEADS + d * DN_HEADS) // 2 + p
    gam_r = rm[ir:ir + 1, :]
    if d == 0:
        incl, strict, g_last = ri >= ci, ri > ci, [g[c - 1:c, :] for g in gam_c]
    else:
        incl, strict, g_last = ri <= ci, ri < ci, [g[0:1, :] for g in gam_c]
    decay = jnp.exp(jnp.where(incl, side(gam_c) - gam_r, -jnp.inf))
    zk = jnp.zeros((c, DN_DIM), BF16)
    kd = jnp.concatenate([jnp.concatenate([k2[:, :DN_DIM], zk], axis=1),
                          jnp.concatenate([zk, k2[:, DN_DIM:]], axis=1)], axis=0)
    g = lax.dot_general(jnp.concatenate([k2, q2], axis=0), kd, (((1,), (1,)), ((), ())),
                        preferred_element_type=F32)
    yield
    kk, qk = g[:c], g[c:]
    lneg = jnp.where(strict, -(side(beta) * kk * decay), 0.0)
    aqk = qk * decay
    eg = [jnp.exp(gc) for gc in gam_c]
    kf = [k2[:, i * DN_DIM:(i + 1) * DN_DIM].astype(F32) for i in range(2)]
    vf = [v2[:, i * DN_DIM:(i + 1) * DN_DIM].astype(F32) for i in range(2)]
    qf = [q2[:, i * DN_DIM:(i + 1) * DN_DIM].astype(F32) for i in range(2)]
    rhs = [jnp.concatenate([(beta[i] * eg[i]) * kf[i], beta[i] * vf[i]], axis=1).astype(BF16) for i in range(2)]
    eye = (ri == ci).astype(F32)
    same_blk = lambda s: (ri // s) == (ci // s)
    dneg = jnp.where(same_blk(4), lneg, 0.0)
    dnb = dneg.astype(BF16)
    dsq = jnp.dot(dnb, _block_diag(dnb), preferred_element_type=F32)
    yield
    t = (eye + dneg) + jnp.dot((eye + dneg).astype(BF16), _block_diag(dsq.astype(BF16)), preferred_element_type=F32)
    yield
    s = 4
    while s < c:
        eneg = jnp.where(jnp.logical_and(same_blk(2 * s), jnp.logical_not(same_blk(s))), lneg, 0.0)
        tb = t.astype(BF16)
        te = jnp.dot(tb, _block_diag(eneg.astype(BF16)), preferred_element_type=F32)
        yield
        t = t + jnp.dot(te.astype(BF16), _block_diag(tb), preferred_element_type=F32)
        yield
        s *= 2
    sol = jnp.dot(_block_diag(t.astype(BF16)), jnp.concatenate(rhs, axis=0), preferred_element_type=F32).astype(BF16)
    yield
    kdec = jnp.concatenate([kf[i] * jnp.exp(g_last[i] - gam_c[i]) for i in range(2)], axis=0)
    lhs = jnp.concatenate([_block_diag(aqk.astype(BF16)), _block_diag(kdec.T.astype(BF16))], axis=0)
    res = jnp.dot(lhs, sol, preferred_element_type=F32)
    yield
    outs, states = [], []
    for i in range(2):
        qo = res[i * c:(i + 1) * c]
        ab = res[2 * c + i * DN_DIM:2 * c + (i + 1) * DN_DIM]
        qeff = qf[i] * eg[i] - qo[:, :DN_DIM]
        both = jnp.dot(jnp.concatenate([qeff, ab[:, :DN_DIM]], axis=0).astype(BF16), s_prev[i].astype(BF16),
                       preferred_element_type=F32)
        outs.append(both[:c] + qo[:, DN_DIM:])
        states.append(jnp.exp(g_last[i]) * s_prev[i] - both[c:] + ab[:, DN_DIM:])
    return outs, states


def _dn_chain_kernel(qf_ref, kf_ref, vf_ref, cf_ref, rf_ref, qb_ref, kb_ref, vb_ref, cb_ref, rb_ref,
                     of_ref, ob_ref, s_ref, *, nb):
    @pl.when(pl.program_id(0) == 0)
    def _():
        s_ref[...] = jnp.zeros_like(s_ref)

    in_refs = ((qf_ref, kf_ref, vf_ref, cf_ref, rf_ref), (qb_ref, kb_ref, vb_ref, cb_ref, rb_ref))
    per_iter = DN_BATCH_UNROLL if nb % DN_BATCH_UNROLL == 0 else 1

    def body(i, carry):
        bs = [i * per_iter + u for u in range(per_iter)]
        n_pairs = DN_HEADS // 2
        gens = [_dn_pair_step(d, p, *[r[b] for r in in_refs[d]], [s_ref[d, b, 2 * p], s_ref[d, b, 2 * p + 1]])
                for b in bs for d in range(2) for p in range(n_pairs)]
        res = _lockstep(gens)
        for ib, b in enumerate(bs):
            for d, o_r in enumerate((of_ref, ob_ref)):
                base = (ib * 2 + d) * n_pairs
                o_r[b] = jnp.concatenate([o for p in range(n_pairs) for o in res[base + p][0]], axis=1)
                for p in range(n_pairs):
                    for i in range(2):
                        s_ref[d, b, 2 * p + i] = res[base + p][1][i]
        return carry

    lax.fori_loop(0, nb // per_iter, body, 0)


def _dn_chain(q, k, v, colsm, rowsm, lc):
    b, t, _ = q.shape
    n_chunks = t // DN_CHUNK
    nc_ctx = lc // DN_CHUNK
    fwd = lambda i: i
    bwd = lambda i: jnp.where(i < nc_ctx, nc_ctx - 1 - i, n_chunks - 1 + nc_ctx - i)
    tok = lambda cols, cm: pl.BlockSpec((b, DN_CHUNK, cols), lambda i: (0, cm(i), 0))
    rowspec = lambda cm: pl.BlockSpec((None, b, 2 * DN_HEADS, 2 * DN_CHUNK), lambda i: (cm(i), 0, 0, 0))
    in_specs = []
    for cm in (fwd, bwd):
        in_specs += [tok(DN_W, cm), tok(DN_W, cm), tok(DN_W, cm), tok(LANES, cm), rowspec(cm)]
    return pl.pallas_call(
        functools.partial(_dn_chain_kernel, nb=b),
        grid=(n_chunks,),
        in_specs=in_specs,
        out_specs=[tok(DN_W, fwd), tok(DN_W, bwd)],
        out_shape=[jax.ShapeDtypeStruct((b, t, DN_W), F32)] * 2,
        scratch_shapes=[pltpu.VMEM((2, b, DN_HEADS, DN_DIM, DN_DIM), F32)],
        compiler_params=pltpu.CompilerParams(dimension_semantics=("arbitrary",), vmem_limit_bytes=VMEM_LIMIT),
        name="dn_chain",
    )(q, k, v, colsm, rowsm, q, k, v, colsm, rowsm)


def _ffn_math(x, g, sc, sh, w1_ref, w2_ref, gt, gf, final_norm):
    h = _norm_mod(x, g, sc, sh).astype(BF16)
    a = jnp.maximum(jnp.dot(h, w1_ref[...], preferred_element_type=F32), 0.0)
    y = jnp.dot((a * a).astype(BF16), w2_ref[...], preferred_element_type=F32)
    out = x + gt * y
    if final_norm:
        out = out * lax.rsqrt(jnp.mean(out * out, axis=-1, keepdims=True) + EPS) * gf
    return out


def _even_out_ffn_kernel(*refs, n_x, final_norm):
    (att_ref, of_ref, ob_ref, gate_ref, gout_ref, w_ref, gt1_ref,
     g_ref, sc_ref, sh_ref, w1_ref, w2_ref, gt2_ref, gf_ref, o_ref) = refs[n_x:]
    o = of_ref[...] + ob_ref[...]
    gate = gate_ref[...].astype(F32)
    parts = [att_ref[...]]
    for h in range(DN_HEADS):
        cs = slice(h * DN_DIM, (h + 1) * DN_DIM)
        oh = o[:, cs]
        yh = oh * lax.rsqrt(jnp.mean(oh * oh, axis=-1, keepdims=True) + EPS) * gout_ref[...]
        parts.append((yh * _silu(gate[:, cs])).astype(BF16))
    a = jnp.concatenate(parts, axis=1)
    y = jnp.dot(a, w_ref[...], preferred_element_type=F32)
    x1 = _stream_tile(refs[:n_x]) + gt1_ref[...] * y
    o_ref[...] = _ffn_math(x1, g_ref[...], sc_ref[...], sh_ref[...], w1_ref, w2_ref, gt2_ref[...], gf_ref[...],
                           final_norm)


def _even_out_ffn(xs, att, o_f, o_b, gate, g_out, w_out, e, gt1, g2, sc2, sh2, w1, w2, l, gt2, g_final, ctx_out,
                  final_norm):
    b, t, d = _stream_shape(xs)
    off = 0 if ctx_out else 1
    assert ctx_out or not isinstance(xs, tuple)
    n_t = t // TM - off
    mod = lambda: _mod_spec(d, ctx_out)
    x_args = _stream_args(xs)
    return pl.pallas_call(
        functools.partial(_even_out_ffn_kernel, n_x=len(x_args), final_norm=final_norm),
        grid=(b, n_t),
        in_specs=_stream_specs(xs, d, off) + [
            _tok_spec(A_W), _tok_spec(DN_W, off), _tok_spec(DN_W, off), _tok_spec(DN_W, off),
            _full_spec((1, DN_DIM)), _layer_spec(w_out, e), mod(),
            _full_spec((1, d)), mod(), mod(), _layer_spec(w1, l), _layer_spec(w2, l), mod(), _full_spec((1, d))],
        out_specs=_tok_spec(d),
        out_shape=jax.ShapeDtypeStruct((b, n_t * TM, d), F32),
        compiler_params=_cparams(2), name="even_out_ffn",
    )(*x_args, att, o_f, o_b, gate, g_out, w_out, gt1, g2, sc2, sh2, w1, w2, gt2, g_final)


def _gelu(x):
    return 0.5 * x * (1.0 + lax.erf(x * (2.0 ** -0.5)))


def _skewed(gens):
    results = [None] * len(gens)
    live, started = [], 0
    while started < len(gens) or live:
        if started < len(gens):
            live.append(started)
            started += 1
        still = []
        for i in reversed(live):
            try:
                next(gens[i])
                still.append(i)
            except StopIteration as stop:
                results[i] = stop.value
        live = still[::-1]
    return results


def _gmlp_kernel(x_ref, g_ref, sc_ref, sh_ref, wi_ref, gv_ref, ws_ref, bs_ref, wo_ref, gt_ref, o_ref, u_ref, v_ref):
    x = x_ref[...]
    h = _norm_mod(x, g_ref[...], sc_ref[...], sh_ref[...]).astype(BF16)
    half = wi_ref.shape[1] // 2
    gw = half // GM_GROUPS

    pw = 2 * gw
    n_pairs = GM_GROUPS // 2

    def project(c0, dst_ref, want_ssq):
        z = jnp.dot(h, wi_ref[:, c0:c0 + pw], preferred_element_type=F32)
        yield
        z = _gelu(z)
        d0 = c0 % half
        dst_ref[:, d0:d0 + pw] = z.astype(dst_ref.dtype)
        return jnp.sum(z * z, axis=-1, keepdims=True) if want_ssq else None

    cols = [(half + p * pw, v_ref, True) for p in range(n_pairs)] + [(p * pw, u_ref, False) for p in range(n_pairs)]
    ssq = functools.reduce(jnp.add, _skewed([project(*c) for c in cols])[:n_pairs])
    rinv = lax.rsqrt(ssq * (1.0 / half) + EPS)

    def mix(p):
        cs = slice(p * pw, (p + 1) * pw)
        vb = (v_ref[:, cs] * rinv * gv_ref[:, cs]).astype(BF16)
        mixed = [[jnp.dot(ws_ref[2 * p + k], vb[n * GM_CHUNK:(n + 1) * GM_CHUNK, k * gw:(k + 1) * gw],
                          preferred_element_type=F32) for k in range(2)] for n in range(TM // GM_CHUNK)]
        yield
        bias = [jnp.concatenate([bs_ref[2 * p + k]] * (gw // LANES), axis=1) for k in range(2)]
        t = jnp.concatenate([jnp.concatenate([row[k] + bias[k] for k in range(2)], axis=1) for row in mixed], axis=0)
        t = (t * u_ref[:, cs]).astype(BF16)
        return jnp.dot(t, wo_ref[cs, :], preferred_element_type=F32)

    y = functools.reduce(jnp.add, _skewed([mix(p) for p in range(n_pairs)]))
    o_ref[...] = x + gt_ref[...] * y


def _gmlp(x, g, sc, sh, w_in, g_v, ws, bs, w_out, o, gt, has_ctx):
    b, t, d = x.shape
    half = w_in.shape[2] // 2
    assert (half // GM_GROUPS) % LANES == 0
    return pl.pallas_call(
        _gmlp_kernel,
        grid=(b, t // TM),
        in_specs=[_tok_spec(d), _full_spec((1, d)), _mod_spec(d, has_ctx), _mod_spec(d, has_ctx),
                  _layer_spec(w_in, o), _full_spec((1, half)), _layer_spec(ws, o), _full_spec(bs.shape),
                  _layer_spec(w_out, o), _mod_spec(d, has_ctx)],
        out_specs=_tok_spec(d),
        out_shape=jax.ShapeDtypeStruct((b, t, d), F32),
        scratch_shapes=[pltpu.VMEM((TM, half), F32), pltpu.VMEM((TM, half), F32)],
        compiler_params=_cparams(2), name="gmlp",
    )(x, g, sc, sh, w_in, g_v, ws, bs, w_out, gt)


def _ffn_kernel(x_ref, g_ref, sc_ref, sh_ref, w1_ref, w2_ref, gt_ref, gf_ref, o_ref, *, final_norm):
    o_ref[...] = _ffn_math(x_ref[...], g_ref[...], sc_ref[...], sh_ref[...], w1_ref, w2_ref, gt_ref[...], gf_ref[...],
                           final_norm)


def _ffn(x, g, sc, sh, w1, w2, l, gt, g_final, has_ctx, final_norm):
    b, t, d = x.shape
    return pl.pallas_call(
        functools.partial(_ffn_kernel, final_norm=final_norm),
        grid=(b, t // TM),
        in_specs=[_tok_spec(d), _full_spec((1, d)), _mod_spec(d, has_ctx), _mod_spec(d, has_ctx),
                  _layer_spec(w1, l), _layer_spec(w2, l), _mod_spec(d, has_ctx), _full_spec((1, d))],
        out_specs=_tok_spec(d),
        out_shape=jax.ShapeDtypeStruct((b, t, d), F32),
        compiler_params=_cparams(2), name="ffn",
    )(x, g, sc, sh, w1, w2, gt, g_final)


def kernel(x, c, ctx, c_ctx, w_ada, b_ada, g_norm_mix, g_norm_ffn, w_in_even, w_out_even, na_rpb, dn_conv, dn_a_log,
           dn_dt_bias, dn_g_out, w_in_odd, gm_g_v, gm_ws, gm_bs, w_out_odd, w_ff1, w_ff2, g_final):
    nb, seq, d = x.shape
    lc = ctx.shape[1]
    depth = w_ada.shape[0]
    assert nb <= 8 and seq % TM == 0 and lc == TM and seq % GRID_W == 0

    cond = jnp.zeros((16, d), F32).at[:nb].set(c).at[8].set(c_ctx)
    mods = _adaln(cond, w_ada, b_ada).reshape(depth, 16, 6, 1, d)
    rope = _rope_tables(lc, seq)
    row = lambda a: a.reshape(1, -1).astype(F32)

    w_in_e = jnp.pad(w_in_even, ((0, 0), (0, 0), (0, LANES - 4 * DN_HEADS))).astype(BF16)
    w_out_e, w_in_o, w_out_o, ws_o = (w.astype(BF16) for w in (w_out_even, w_in_odd, w_out_odd, gm_ws))
    w1, w2 = w_ff1.astype(BF16), w_ff2.astype(BF16)

    xs = (ctx, x)
    has_ctx = True
    for l in range(depth):
        ctx_live = any(j % 2 == 0 for j in range(l + 1, depth))
        sh1, sc1, gt1, sh2, sc2, gt2 = [mods[l, :, i] for i in range(6)]
        if l % 2 == 0:
            assert has_ctx
            e = l // 2
            qkv_a, qkv_b, gate, small = _even_in(xs, row(g_norm_mix[l]), sc1, sh1, w_in_e, e)
            att = _na_attention(qkv_a, _na_bias_table(na_rpb[e]), lc, ctx_live)
            q, k, v, colsm, rowsm = _dn_prep(qkv_b, small, dn_conv[e], dn_a_log[e], dn_dt_bias[e], rope)
            o_f, o_b = _dn_chain(q, k, v, colsm, rowsm, lc)
            if isinstance(xs, tuple) and not ctx_live:
                xs = jnp.concatenate(xs, axis=1)
            xs = _even_out_ffn(xs, att, o_f, o_b, gate, row(dn_g_out[e]), w_out_e, e, gt1,
                               row(g_norm_ffn[l]), sc2, sh2, w1, w2, l, gt2, row(g_final), ctx_live, l == depth - 1)
            has_ctx = ctx_live
        else:
            if isinstance(xs, tuple):
                xs = jnp.concatenate(xs, axis=1)
            o = l // 2
            bs = jnp.broadcast_to(gm_bs[o].astype(F32)[:, :, None], (GM_GROUPS, GM_CHUNK, LANES))
            xs = _gmlp(xs, row(g_norm_mix[l]), sc1, sh1, w_in_o, row(gm_g_v[o]), ws_o, bs, w_out_o, o, gt1, has_ctx)
            xs = _ffn(xs, row(g_norm_ffn[l]), sc2, sh2, w1, w2, l, gt2, row(g_final), has_ctx, l == depth - 1)
    return xs[:, lc:] if has_ctx else xs
```

```python
import functools

import jax
import jax.numpy as jnp
from jax import lax
from jax.experimental import pallas as pl
from jax.experimental.pallas import tpu as pltpu

EPS = 1e-6
GRID_W = 64
NA_HEADS, NA_DIM, NA_WIN_R, NA_WIN_C = 8, 64, 8, 16
A_W = NA_HEADS * NA_DIM
DN_HEADS, DN_DIM, DN_CHUNK = 4, 128, 64
DN_W = DN_HEADS * DN_DIM
ROPE_BASE = 10000.0
GM_GROUPS, GM_CHUNK = 8, 128

LANES = 128
TM = 256
DN_BATCH_UNROLL = 4
NA_ROWS_PER_STEP = 4
NA_CHAINS_IN_FLIGHT = 8
VMEM_LIMIT = 56 * 1024 * 1024
F32, BF16 = jnp.float32, jnp.bfloat16


def _cparams(n_axes):
    return pltpu.CompilerParams(dimension_semantics=("parallel",) * n_axes, vmem_limit_bytes=VMEM_LIMIT)


def _tok_spec(cols, off=0, col_block=0):
    return pl.BlockSpec((None, TM, cols), lambda b, j: (b, j + off, col_block))


MOD_ROWS = 16
CTX_ROW = 8
SH1, SC1, GT1, SH2, SC2, GT2 = range(6)


def _mod_spec(d, ctx_tile, l, kind):
    if ctx_tile:
        return pl.BlockSpec((None, None, None, 1, d), lambda b, j: (l, kind, jnp.where(j == 0, CTX_ROW, b), 0, 0))
    return pl.BlockSpec((None, None, None, 1, d), lambda b, j: (l, kind, b, 0, 0))


def _full_spec(shape):
    n = len(shape)
    return pl.BlockSpec(shape, lambda b, j: (0,) * n)


def _layer_spec(w, l):
    return pl.BlockSpec((None,) + w.shape[1:], lambda b, j: (l,) + (0,) * (w.ndim - 1))


def _stream_specs(xs, d, off=0):
    if isinstance(xs, tuple):
        return [pl.BlockSpec((None, TM, d), lambda b, j: (b, 0, 0)),
                pl.BlockSpec((None, TM, d), lambda b, j: (b, jnp.maximum(j - 1, 0), 0))]
    return [_tok_spec(d, off)]


def _stream_tile(refs):
    if len(refs) == 2:
        return jnp.where(pl.program_id(1) == 0, refs[0][...], refs[1][...])
    return refs[0][...]


def _stream_args(xs):
    return list(xs) if isinstance(xs, tuple) else [xs]


def _stream_shape(xs):
    if isinstance(xs, tuple):
        return xs[1].shape[0], xs[0].shape[1] + xs[1].shape[1], xs[1].shape[2]
    return xs.shape


def _norm_mod(x, g, sc, sh):
    y = x * lax.rsqrt(jnp.mean(x * x, axis=-1, keepdims=True) + EPS)
    return (y * g) * (1.0 + sc) + sh


def _silu(x):
    return x * jax.nn.sigmoid(x)


def _adaln_kernel(cond_ref, w_ref, b_ref, o_ref):
    a = _silu(cond_ref[...]).astype(BF16)
    m = jnp.dot(a, w_ref[...].astype(BF16), preferred_element_type=F32) + b_ref[...]
    for r in range(MOD_ROWS):
        o_ref[r] = m[r:r + 1, :]


def _adaln(cond, w_ada, b_ada):
    depth, d, n = w_ada.shape
    assert n == 6 * d
    return pl.pallas_call(
        _adaln_kernel,
        grid=(depth, 6),
        in_specs=[pl.BlockSpec((MOD_ROWS, d), lambda l, j: (0, 0)),
                  pl.BlockSpec((None, d, d), lambda l, j: (l, 0, j)),
                  pl.BlockSpec((None, 1, d), lambda l, j: (l, 0, j))],
        out_specs=pl.BlockSpec((None, None, MOD_ROWS, 1, d), lambda l, j: (l, j, 0, 0, 0)),
        out_shape=jax.ShapeDtypeStruct((depth, 6, MOD_ROWS, 1, d), F32),
        compiler_params=_cparams(2), name="adaln",
    )(cond, w_ada, b_ada.reshape(depth, 1, n))


def _even_in_kernel(*refs, n_x):
    g_ref, sc_ref, sh_ref, w_ref, qa_ref, qb_ref, gate_ref, sm_ref = refs[n_x:]
    h = _norm_mod(_stream_tile(refs[:n_x]), g_ref[...], sc_ref[...], sh_ref[...]).astype(BF16)
    p = jnp.dot(h, w_ref[...], preferred_element_type=F32)
    qa_ref[...] = p[:, :3 * A_W].astype(BF16)
    qb_ref[...] = p[:, 3 * A_W:3 * A_W + 3 * DN_W].astype(BF16)
    gate_ref[...] = p[:, 3 * A_W + 3 * DN_W:3 * A_W + 4 * DN_W].astype(BF16)
    sm_ref[...] = p[:, 3 * A_W + 4 * DN_W:]


def _even_in(xs, g, mods, l, w_pad, e):
    b, t, d = _stream_shape(xs)
    outs = [(3 * A_W, BF16), (3 * DN_W, BF16), (DN_W, BF16), (LANES, F32)]
    x_args = _stream_args(xs)
    return pl.pallas_call(
        functools.partial(_even_in_kernel, n_x=len(x_args)),
        grid=(b, t // TM),
        in_specs=_stream_specs(xs, d) + [_full_spec((1, d)), _mod_spec(d, True, l, SC1), _mod_spec(d, True, l, SH1),
                                         _layer_spec(w_pad, e)],
        out_specs=[_tok_spec(c) for c, _ in outs],
        out_shape=[jax.ShapeDtypeStruct((b, t, c), dt) for c, dt in outs],
        compiler_params=_cparams(2), name="even_in",
    )(*x_args, g, mods, mods, w_pad)


def _lockstep(gens):
    results = [None] * len(gens)
    live = list(range(len(gens)))
    while live:
        still = []
        for i in live:
            try:
                next(gens[i])
                still.append(i)
            except StopIteration as stop:
                results[i] = stop.value
        live = still
    return results


def _softmax_head_pair(q, keys, vals, biases):
    wq = q.shape[0]
    lane = lax.broadcasted_iota(jnp.int32, (2 * wq, LANES), 1)
    row = lax.broadcasted_iota(jnp.int32, (2 * wq, LANES), 0)
    own = (lane // NA_DIM) == (row // wq)
    q2 = jnp.concatenate([q, q], axis=0)
    qm = jnp.where(own, q2, jnp.zeros_like(q2))
    ss = [lax.dot_general(qm, kb, (((1,), (1,)), ((), ())), preferred_element_type=F32) for kb in keys]
    yield
    ss = [s if bias is None else s + bias for s, bias in zip(ss, biases)]
    m = functools.reduce(jnp.maximum, [jnp.max(s, axis=-1, keepdims=True) for s in ss])
    yield
    ps = [jnp.exp(s - m) for s in ss]
    den = functools.reduce(jnp.add, [jnp.sum(p, axis=-1, keepdims=True) for p in ps])
    yield
    o = functools.reduce(jnp.add, [jnp.dot(p.astype(BF16), vb, preferred_element_type=F32)
                                   for p, vb in zip(ps, vals)])
    yield
    o = o / den
    return jnp.where(lax.broadcasted_iota(jnp.int32, (wq, LANES), 1) < NA_DIM, o[:wq], o[wq:])


def _attend_jobs(q_ref, o_ref, jobs):
    scale = NA_DIM ** -0.5
    gens, dests = [], []
    for rsl, blocks, bias_of in jobs:
        for pair in range(NA_HEADS // 2):
            cs = slice(pair * LANES, (pair + 1) * LANES)
            keys, vals = blocks(cs)
            gens.append(_softmax_head_pair(q_ref[rsl, cs] * scale, keys, vals, bias_of(pair)))
            dests.append((rsl, cs))
    for g0 in range(0, len(gens), NA_CHAINS_IN_FLIGHT):
        outs = _lockstep(gens[g0:g0 + NA_CHAINS_IN_FLIGHT])
        for (rsl, cs), o in zip(dests[g0:g0 + NA_CHAINS_IN_FLIGHT], outs):
            o_ref[rsl, cs] = o.astype(o_ref.dtype)


def _na_kernel(q_ref, k_ref, v_ref, bias_ref, o_ref, *, lc, rows, ctx_steps):
    step = pl.program_id(1)
    n_loc = NA_WIN_R * GRID_W

    def latent_rows():
        jobs = []
        for i in range(NA_ROWS_PER_STEP):
            r = (step - ctx_steps) * NA_ROWS_PER_STEP + i
            rs = jnp.clip(r - NA_WIN_R // 2, 0, rows - NA_WIN_R)
            start = pl.multiple_of(lc + rs * GRID_W, GRID_W)
            blocks = lambda cs, start=start: ([k_ref[pl.ds(start, n_loc), cs], k_ref[0:lc, cs]],
                                              [v_ref[pl.ds(start, n_loc), cs], v_ref[0:lc, cs]])
            bias_of = lambda pair, var=r - rs: [bias_ref[var, 2 * pair:2 * pair + 2].reshape(2 * GRID_W, n_loc), None]
            jobs.append((slice(i * GRID_W, (i + 1) * GRID_W), blocks, bias_of))
        _attend_jobs(q_ref, o_ref, jobs)

    def context_block():
        blocks = lambda cs: ([k_ref[0:lc, cs]], [v_ref[0:lc, cs]])
        _attend_jobs(q_ref, o_ref, [(slice(None), blocks, lambda pair: [None])])

    if ctx_steps:
        pl.when(step < ctx_steps)(context_block)
        pl.when(step >= ctx_steps)(latent_rows)
    else:
        latent_rows()


def _na_bias_table(rpb):
    col = jnp.arange(GRID_W)
    cstart = jnp.clip(col - NA_WIN_C // 2, 0, GRID_W - NA_WIN_C)
    col_ok = (col[None, :] >= cstart[:, None]) & (col[None, :] < cstart[:, None] + NA_WIN_C)
    dc_idx = jnp.clip(col[None, :] - col[:, None], -(NA_WIN_C - 1), NA_WIN_C - 1) + NA_WIN_C - 1
    onehot = (dc_idx[:, :, None] == jnp.arange(2 * NA_WIN_C - 1)).astype(F32)
    v_i, w_i = jnp.arange(NA_WIN_R)[:, None], jnp.arange(NA_WIN_R)[None, :]
    rowhot = ((NA_WIN_R - 1 - v_i + w_i)[:, :, None] == jnp.arange(2 * NA_WIN_R - 1)).astype(F32)
    tab = jnp.einsum('hij,vwi,qkj->vhqwk', rpb.astype(F32), rowhot, onehot, precision=lax.Precision.HIGHEST)
    tab = jnp.where(col_ok[None, None, :, None, :], tab, -jnp.inf)
    return tab.reshape(NA_WIN_R, NA_HEADS, GRID_W, NA_WIN_R * GRID_W)


def _na_attention(qkv_a, bias_tab, lc, ctx_out):
    b, t, _ = qkv_a.shape
    rows = (t - lc) // GRID_W
    tq = NA_ROWS_PER_STEP * GRID_W
    assert rows >= NA_WIN_R and rows % NA_ROWS_PER_STEP == 0 and lc % tq == 0
    ctx_steps = lc // tq if ctx_out else 0
    q_off = 0 if ctx_out else lc // tq
    n_out = t if ctx_out else t - lc
    return pl.pallas_call(
        functools.partial(_na_kernel, lc=lc, rows=rows, ctx_steps=ctx_steps),
        grid=(b, ctx_steps + rows // NA_ROWS_PER_STEP),
        in_specs=[pl.BlockSpec((None, tq, A_W), lambda bi, s: (bi, s + q_off, 0)),
                  pl.BlockSpec((None, t, A_W), lambda bi, s: (bi, 0, 1)),
                  pl.BlockSpec((None, t, A_W), lambda bi, s: (bi, 0, 2)),
                  pl.BlockSpec(bias_tab.shape, lambda bi, s: (0, 0, 0, 0))],
        out_specs=pl.BlockSpec((None, tq, A_W), lambda bi, s: (bi, s, 0)),
        out_shape=jax.ShapeDtypeStruct((b, n_out, A_W), BF16),
        compiler_params=_cparams(2), name="na_attention",
    )(qkv_a, qkv_a, qkv_a, bias_tab)


def _chunk_scans(x):
    n = x.shape[0]
    pos = lax.broadcasted_iota(jnp.int32, x.shape, 0) % DN_CHUNK
    fwd, s = x, 1
    while s < DN_CHUNK:
        fwd = fwd + jnp.where(pos >= s, pltpu.roll(fwd, s, 0), 0.0)
        s *= 2
    grouped = fwd.reshape(n // DN_CHUNK, DN_CHUNK, x.shape[1])
    total = jnp.broadcast_to(grouped[:, DN_CHUNK - 1:DN_CHUNK, :], grouped.shape).reshape(x.shape)
    return fwd, total - fwd + x


def _conv_shift_matrix():
    r = lax.broadcasted_iota(jnp.int32, (2 * TM, TM + LANES), 0)
    c = lax.broadcasted_iota(jnp.int32, (2 * TM, TM + LANES), 1)
    down = (r < TM) & ((c == r - 1) | ((r == 0) & (c == TM + 15)))
    up = (r >= TM) & (((c == r - TM + 1) & (r < 2 * TM - 1)) | ((r == 2 * TM - 1) & (c == TM + 16)))
    return (down | up).astype(BF16)


def _dn_prep_kernel(x_ref, prev_ref, next_ref, sm_ref, shift_ref, cw_ref, cos_ref, sina_ref, sinb_ref, alog_ref,
                    dtb_ref, q_ref, k_ref, v_ref, col_ref, row_ref, *, n_tiles):
    j = pl.program_id(1)
    xb = x_ref[...]
    has_prev = jnp.logical_and(j != 0, j != 1)
    has_next = jnp.logical_and(j != 0, j != n_tiles - 1)
    halo_p = jnp.where(has_prev, prev_ref[...], jnp.zeros(prev_ref.shape, prev_ref.dtype))
    halo_n = jnp.where(has_next, next_ref[...], jnp.zeros(next_ref.shape, next_ref.dtype))
    ext = jnp.concatenate([xb, halo_p, halo_n, jnp.zeros((LANES - 32, xb.shape[1]), xb.dtype)], axis=0)
    shifted = jnp.dot(shift_ref[...], ext, preferred_element_type=F32)
    xm, xp = shifted[:TM], shifted[TM:]
    y = _silu(cw_ref[0:1, :] * xm + cw_ref[1:2, :] * xb.astype(F32) + cw_ref[2:3, :] * xp)

    cos, sina, sinb = cos_ref[...], sina_ref[...], sinb_ref[...]
    for h in range(DN_HEADS):
        for base, o_ref, scale in ((0, q_ref, DN_DIM ** -0.5), (DN_W, k_ref, None)):
            cs = slice(base + h * DN_DIM, base + (h + 1) * DN_DIM)
            z = y[:, cs]
            z = z * lax.rsqrt(jnp.sum(z * z, axis=-1, keepdims=True) + EPS)
            z = z * cos + pltpu.roll(z, 3 * DN_DIM // 4, 1) * sina + pltpu.roll(z, DN_DIM // 4, 1) * sinb
            if scale is not None:
                z = z * scale
            o_ref[:, h * DN_DIM:(h + 1) * DN_DIM] = z.astype(o_ref.dtype)
    v_ref[...] = y[:, 2 * DN_W:].astype(v_ref.dtype)

    sm = sm_ref[...]
    lane = lax.broadcasted_iota(jnp.int32, sm.shape, 1)
    logg = -jnp.exp(alog_ref[...]) * jax.nn.softplus(sm + dtb_ref[...])
    gam = jnp.where(lane < 3 * DN_HEADS, *_chunk_scans(logg))
    col = jnp.where(lane < 2 * DN_HEADS, jax.nn.sigmoid(sm), gam)
    col_ref[...] = col
    row_ref[...] = col.T[0:4 * DN_HEADS, :]


def _rope_tables(lc, seq):
    t = jnp.arange(seq)
    rowp = (t // GRID_W).astype(F32)
    colp = (t % GRID_W).astype(F32)
    n_freq = DN_DIM // 4
    inv = ROPE_BASE ** (-jnp.arange(n_freq, dtype=F32) / n_freq)
    ar, ac = rowp[:, None] * inv, colp[:, None] * inv
    ang = jnp.concatenate([ar, ar, ac, ac], axis=-1)
    cos, sin = jnp.cos(ang), jnp.sin(ang)
    quarter = (jnp.arange(DN_DIM) // n_freq) % 2
    sina = jnp.where(quarter == 0, -sin, 0.0)
    sinb = jnp.where(quarter == 1, sin, 0.0)
    pad = lambda a, v: jnp.concatenate([jnp.full((lc, DN_DIM), v, F32), a], axis=0)
    return pad(cos, 1.0), pad(sina, 0.0), pad(sinb, 0.0)


def _dn_prep(qkv_b, small, conv_w, a_log, dt_bias, rope):
    b, t, _ = qkv_b.shape
    n_tiles = t // TM
    hb = TM // 16
    pad16 = lambda a: jnp.zeros((1, LANES), F32).at[0, 2 * DN_HEADS:4 * DN_HEADS].set(a.reshape(-1).astype(F32))
    outs = [(DN_W, BF16), (DN_W, BF16), (DN_W, BF16), (LANES, F32)]
    res = pl.pallas_call(
        functools.partial(_dn_prep_kernel, n_tiles=n_tiles),
        grid=(b, n_tiles),
        in_specs=[_tok_spec(3 * DN_W),
                  pl.BlockSpec((None, 16, 3 * DN_W), lambda bi, j: (bi, jnp.maximum(j * hb - 1, 0), 0)),
                  pl.BlockSpec((None, 16, 3 * DN_W), lambda bi, j: (bi, jnp.minimum((j + 1) * hb, t // 16 - 1), 0)),
                  _tok_spec(LANES),
                  _full_spec((2 * TM, TM + LANES)),
                  _full_spec((3, 3 * DN_W)),
                  pl.BlockSpec((TM, DN_DIM), lambda bi, j: (j, 0)),
                  pl.BlockSpec((TM, DN_DIM), lambda bi, j: (j, 0)),
                  pl.BlockSpec((TM, DN_DIM), lambda bi, j: (j, 0)),
                  _full_spec((1, LANES)), _full_spec((1, LANES))],
        out_specs=[_tok_spec(c) for c, _ in outs] + [pl.BlockSpec((None, 4 * DN_HEADS, TM), lambda bi, j: (bi, 0, j))],
        out_shape=[jax.ShapeDtypeStruct((b, t, c), dt) for c, dt in outs]
        + [jax.ShapeDtypeStruct((b, 4 * DN_HEADS, t), F32)],
        compiler_params=_cparams(2), name="dn_prep",
    )(qkv_b, qkv_b, qkv_b, small, _conv_shift_matrix(), conv_w.astype(F32), *rope, pad16(a_log), pad16(dt_bias))
    q, k, v, colsm, rowsm = res
    n_chunks = t // DN_CHUNK
    rowsm = rowsm.reshape(b, 4 * DN_HEADS, n_chunks, DN_CHUNK).transpose(2, 0, 1, 3)
    rowsm = rowsm.reshape(n_chunks, b, 2 * DN_HEADS, 2 * DN_CHUNK)
    return q, k, v, colsm, rowsm


def _block_diag(x):
    n, w = x.shape
    x2 = jnp.concatenate([x, x], axis=0)
    row = lax.broadcasted_iota(jnp.int32, x2.shape, 0)
    lane = lax.broadcasted_iota(jnp.int32, x2.shape, 1)
    return jnp.where((row // n) == (lane // (w // 2)), x2, jnp.zeros_like(x2))


def _dn_pair_step(d, p, q_all, k_all, v_all, cm, rm, s_prev):
    c = DN_CHUNK
    heads = (2 * p, 2 * p + 1)
    cs2 = slice(2 * p * DN_DIM, (2 * p + 2) * DN_DIM)
    q2, k2, v2 = q_all[:, cs2], k_all[:, cs2], v_all[:, cs2]
    lane = lax.broadcasted_iota(jnp.int32, (c, 2 * c), 1)
    ri = lax.broadcasted_iota(jnp.int32, (c, 2 * c), 0)
    ci = lane % c
    first = lane < c
    col = lambda base: [cm[:, base + h:base + h + 1] for h in heads]
    beta, gam_c = col(d * DN_HEADS), col(2 * DN_HEADS + d * DN_HEADS)
    side = lambda ab: jnp.where(first, ab[0], ab[1])
    ir = (2 * DN_HEADS + d * DN_HEADS) // 2 + p
    gam_r = rm[ir:ir + 1, :]
    if d == 0:
        incl, strict, g_last = ri >= ci, ri > ci, [g[c - 1:c, :] for g in gam_c]
    else:
        incl, strict, g_last = ri <= ci, ri < ci, [g[0:1, :] for g in gam_c]
    decay = jnp.exp(jnp.where(incl, side(gam_c) - gam_r, -jnp.inf))
    zk = jnp.zeros((c, DN_DIM), BF16)
    kd = jnp.concatenate([jnp.concatenate([k2[:, :DN_DIM], zk], axis=1),
                          jnp.concatenate([zk, k2[:, DN_DIM:]], axis=1)], axis=0)
    g = lax.dot_general(jnp.concatenate([k2, q2], axis=0), kd, (((1,), (1,)), ((), ())),
                        preferred_element_type=F32)
    yield
    kk, qk = g[:c], g[c:]
    lneg = jnp.where(strict, -(side(beta) * kk * decay), 0.0)
    aqk = qk * decay
    eg = [jnp.exp(gc) for gc in gam_c]
    kf = [k2[:, i * DN_DIM:(i + 1) * DN_DIM].astype(F32) for i in range(2)]
    vf = [v2[:, i * DN_DIM:(i + 1) * DN_DIM].astype(F32) for i in range(2)]
    qf = [q2[:, i * DN_DIM:(i + 1) * DN_DIM].astype(F32) for i in range(2)]
    rhs = [jnp.concatenate([(beta[i] * eg[i]) * kf[i], beta[i] * vf[i]], axis=1).astype(BF16) for i in range(2)]
    eye = (ri == ci).astype(F32)
    same_blk = lambda s: (ri // s) == (ci // s)
    dneg = jnp.where(same_blk(4), lneg, 0.0)
    dnb = dneg.astype(BF16)
    dsq = jnp.dot(dnb, _block_diag(dnb), preferred_element_type=F32)
    yield
    t = (eye + dneg) + jnp.dot((eye + dneg).astype(BF16), _block_diag(dsq.astype(BF16)), preferred_element_type=F32)
    yield
    s = 4
    while s < c:
        eneg = jnp.where(jnp.logical_and(same_blk(2 * s), jnp.logical_not(same_blk(s))), lneg, 0.0)
        tb = t.astype(BF16)
        te = jnp.dot(tb, _block_diag(eneg.astype(BF16)), preferred_element_type=F32)
        yield
        t = t + jnp.dot(te.astype(BF16), _block_diag(tb), preferred_element_type=F32)
        yield
        s *= 2
    sol = jnp.dot(_block_diag(t.astype(BF16)), jnp.concatenate(rhs, axis=0), preferred_element_type=F32).astype(BF16)
    yield
    kdec = jnp.concatenate([kf[i] * jnp.exp(g_last[i] - gam_c[i]) for i in range(2)], axis=0)
    lhs = jnp.concatenate([_block_diag(aqk.astype(BF16)), _block_diag(kdec.T.astype(BF16))], axis=0)
    res = jnp.dot(lhs, sol, preferred_element_type=F32)
    yield
    outs, states = [], []
    for i in range(2):
        qo = res[i * c:(i + 1) * c]
        ab = res[2 * c + i * DN_DIM:2 * c + (i + 1) * DN_DIM]
        qeff = qf[i] * eg[i] - qo[:, :DN_DIM]
        both = jnp.dot(jnp.concatenate([qeff, ab[:, :DN_DIM]], axis=0).astype(BF16), s_prev[i].astype(BF16),
                       preferred_element_type=F32)
        outs.append(both[:c] + qo[:, DN_DIM:])
        states.append(jnp.exp(g_last[i]) * s_prev[i] - both[c:] + ab[:, DN_DIM:])
    return outs, states


def _dn_chain_kernel(qf_ref, kf_ref, vf_ref, cf_ref, rf_ref, qb_ref, kb_ref, vb_ref, cb_ref, rb_ref,
                     of_ref, ob_ref, s_ref, *, nb):
    @pl.when(pl.program_id(0) == 0)
    def _():
        s_ref[...] = jnp.zeros_like(s_ref)

    in_refs = ((qf_ref, kf_ref, vf_ref, cf_ref, rf_ref), (qb_ref, kb_ref, vb_ref, cb_ref, rb_ref))
    per_iter = DN_BATCH_UNROLL if nb % DN_BATCH_UNROLL == 0 else 1

    def body(i, carry):
        bs = [i * per_iter + u for u in range(per_iter)]
        n_pairs = DN_HEADS // 2
        gens = [_dn_pair_step(d, p, *[r[b] for r in in_refs[d]], [s_ref[d, b, 2 * p], s_ref[d, b, 2 * p + 1]])
                for b in bs for d in range(2) for p in range(n_pairs)]
        res = _lockstep(gens)
        for ib, b in enumerate(bs):
            for d, o_r in enumerate((of_ref, ob_ref)):
                base = (ib * 2 + d) * n_pairs
                o_r[b] = jnp.concatenate([o for p in range(n_pairs) for o in res[base + p][0]], axis=1)
                for p in range(n_pairs):
                    for i in range(2):
                        s_ref[d, b, 2 * p + i] = res[base + p][1][i]
        return carry

    lax.fori_loop(0, nb // per_iter, body, 0)


def _dn_chain(q, k, v, colsm, rowsm, lc):
    b, t, _ = q.shape
    n_chunks = t // DN_CHUNK
    nc_ctx = lc // DN_CHUNK
    fwd = lambda i: i
    bwd = lambda i: jnp.where(i < nc_ctx, nc_ctx - 1 - i, n_chunks - 1 + nc_ctx - i)
    tok = lambda cols, cm: pl.BlockSpec((b, DN_CHUNK, cols), lambda i: (0, cm(i), 0))
    rowspec = lambda cm: pl.BlockSpec((None, b, 2 * DN_HEADS, 2 * DN_CHUNK), lambda i: (cm(i), 0, 0, 0))
    in_specs = []
    for cm in (fwd, bwd):
        in_specs += [tok(DN_W, cm), tok(DN_W, cm), tok(DN_W, cm), tok(LANES, cm), rowspec(cm)]
    return pl.pallas_call(
        functools.partial(_dn_chain_kernel, nb=b),
        grid=(n_chunks,),
        in_specs=in_specs,
        out_specs=[tok(DN_W, fwd), tok(DN_W, bwd)],
        out_shape=[jax.ShapeDtypeStruct((b, t, DN_W), F32)] * 2,
        scratch_shapes=[pltpu.VMEM((2, b, DN_HEADS, DN_DIM, DN_DIM), F32)],
        compiler_params=pltpu.CompilerParams(dimension_semantics=("arbitrary",), vmem_limit_bytes=VMEM_LIMIT),
        name="dn_chain",
    )(q, k, v, colsm, rowsm, q, k, v, colsm, rowsm)


def _ffn_math(x, g, sc, sh, w1_ref, w2_ref, gt, gf, final_norm):
    h = _norm_mod(x, g, sc, sh).astype(BF16)
    a = jnp.maximum(jnp.dot(h, w1_ref[...], preferred_element_type=F32), 0.0)
    y = jnp.dot((a * a).astype(BF16), w2_ref[...], preferred_element_type=F32)
    out = x + gt * y
    if final_norm:
        out = out * lax.rsqrt(jnp.mean(out * out, axis=-1, keepdims=True) + EPS) * gf
    return out


def _even_out_ffn_kernel(*refs, n_x, final_norm):
    (att_ref, of_ref, ob_ref, gate_ref, gout_ref, w_ref, gt1_ref,
     g_ref, sc_ref, sh_ref, w1_ref, w2_ref, gt2_ref, gf_ref, o_ref) = refs[n_x:]
    o = of_ref[...] + ob_ref[...]
    gate = gate_ref[...].astype(F32)
    parts = [att_ref[...]]
    for h in range(DN_HEADS):
        cs = slice(h * DN_DIM, (h + 1) * DN_DIM)
        oh = o[:, cs]
        yh = oh * lax.rsqrt(jnp.mean(oh * oh, axis=-1, keepdims=True) + EPS) * gout_ref[...]
        parts.append((yh * _silu(gate[:, cs])).astype(BF16))
    a = jnp.concatenate(parts, axis=1)
    y = jnp.dot(a, w_ref[...], preferred_element_type=F32)
    x1 = _stream_tile(refs[:n_x]) + gt1_ref[...] * y
    o_ref[...] = _ffn_math(x1, g_ref[...], sc_ref[...], sh_ref[...], w1_ref, w2_ref, gt2_ref[...], gf_ref[...],
                           final_norm)


def _even_out_ffn(xs, att, o_f, o_b, gate, g_out, w_out, e, g2, mods, w1, w2, l, g_final, ctx_out, final_norm):
    b, t, d = _stream_shape(xs)
    off = 0 if ctx_out else 1
    assert ctx_out or not isinstance(xs, tuple)
    n_t = t // TM - off
    mod = lambda kind: _mod_spec(d, ctx_out, l, kind)
    x_args = _stream_args(xs)
    return pl.pallas_call(
        functools.partial(_even_out_ffn_kernel, n_x=len(x_args), final_norm=final_norm),
        grid=(b, n_t),
        in_specs=_stream_specs(xs, d, off) + [
            _tok_spec(A_W), _tok_spec(DN_W, off), _tok_spec(DN_W, off), _tok_spec(DN_W, off),
            _full_spec((1, DN_DIM)), _layer_spec(w_out, e), mod(GT1),
            _full_spec((1, d)), mod(SC2), mod(SH2), _layer_spec(w1, l), _layer_spec(w2, l), mod(GT2),
            _full_spec((1, d))],
        out_specs=_tok_spec(d),
        out_shape=jax.ShapeDtypeStruct((b, n_t * TM, d), F32),
        compiler_params=_cparams(2), name="even_out_ffn",
    )(*x_args, att, o_f, o_b, gate, g_out, w_out, mods, g2, mods, mods, w1, w2, mods, g_final)


def _gelu(x):
    return 0.5 * x * (1.0 + lax.erf(x * (2.0 ** -0.5)))


def _skewed(gens):
    results = [None] * len(gens)
    live, started = [], 0
    while started < len(gens) or live:
        if started < len(gens):
            live.append(started)
            started += 1
        still = []
        for i in reversed(live):
            try:
                next(gens[i])
                still.append(i)
            except StopIteration as stop:
                results[i] = stop.value
        live = still[::-1]
    return results


def _gmlp_kernel(x_ref, g_ref, sc_ref, sh_ref, wi_ref, gv_ref, ws_ref, bs_ref, wo_ref, gt_ref, o_ref, u_ref, v_ref):
    x = x_ref[...]
    h = _norm_mod(x, g_ref[...], sc_ref[...], sh_ref[...]).astype(BF16)
    half = wi_ref.shape[1] // 2
    gw = half // GM_GROUPS

    pw = 2 * gw
    n_pairs = GM_GROUPS // 2

    def project(c0, dst_ref, want_ssq):
        z = jnp.dot(h, wi_ref[:, c0:c0 + pw], preferred_element_type=F32)
        yield
        z = _gelu(z)
        d0 = c0 % half
        dst_ref[:, d0:d0 + pw] = z.astype(dst_ref.dtype)
        return jnp.sum(z * z, axis=-1, keepdims=True) if want_ssq else None

    cols = [(half + p * pw, v_ref, True) for p in range(n_pairs)] + [(p * pw, u_ref, False) for p in range(n_pairs)]
    ssq = functools.reduce(jnp.add, _skewed([project(*c) for c in cols])[:n_pairs])
    rinv = lax.rsqrt(ssq * (1.0 / half) + EPS)

    def mix(p):
        cs = slice(p * pw, (p + 1) * pw)
        vb = (v_ref[:, cs] * rinv * gv_ref[:, cs]).astype(BF16)
        mixed = [[jnp.dot(ws_ref[2 * p + k], vb[n * GM_CHUNK:(n + 1) * GM_CHUNK, k * gw:(k + 1) * gw],
                          preferred_element_type=F32) for k in range(2)] for n in range(TM // GM_CHUNK)]
        yield
        bias = [jnp.concatenate([bs_ref[2 * p + k]] * (gw // LANES), axis=1) for k in range(2)]
        t = jnp.concatenate([jnp.concatenate([row[k] + bias[k] for k in range(2)], axis=1) for row in mixed], axis=0)
        t = (t * u_ref[:, cs]).astype(BF16)
        return jnp.dot(t, wo_ref[cs, :], preferred_element_type=F32)

    y = functools.reduce(jnp.add, _skewed([mix(p) for p in range(n_pairs)]))
    o_ref[...] = x + gt_ref[...] * y


def _gmlp(x, g, mods, l, w_in, g_v, ws, bs, w_out, o, has_ctx):
    b, t, d = x.shape
    half = w_in.shape[2] // 2
    assert (half // GM_GROUPS) % LANES == 0
    return pl.pallas_call(
        _gmlp_kernel,
        grid=(b, t // TM),
        in_specs=[_tok_spec(d), _full_spec((1, d)), _mod_spec(d, has_ctx, l, SC1), _mod_spec(d, has_ctx, l, SH1),
                  _layer_spec(w_in, o), _full_spec((1, half)), _layer_spec(ws, o), _full_spec(bs.shape),
                  _layer_spec(w_out, o), _mod_spec(d, has_ctx, l, GT1)],
        out_specs=_tok_spec(d),
        out_shape=jax.ShapeDtypeStruct((b, t, d), F32),
        scratch_shapes=[pltpu.VMEM((TM, half), F32), pltpu.VMEM((TM, half), F32)],
        compiler_params=_cparams(2), name="gmlp",
    )(x, g, mods, mods, w_in, g_v, ws, bs, w_out, mods)


def _ffn_kernel(x_ref, g_ref, sc_ref, sh_ref, w1_ref, w2_ref, gt_ref, gf_ref, o_ref, *, final_norm):
    o_ref[...] = _ffn_math(x_ref[...], g_ref[...], sc_ref[...], sh_ref[...], w1_ref, w2_ref, gt_ref[...], gf_ref[...],
                           final_norm)


def _ffn(x, g, mods, w1, w2, l, g_final, has_ctx, final_norm):
    b, t, d = x.shape
    return pl.pallas_call(
        functools.partial(_ffn_kernel, final_norm=final_norm),
        grid=(b, t // TM),
        in_specs=[_tok_spec(d), _full_spec((1, d)), _mod_spec(d, has_ctx, l, SC2), _mod_spec(d, has_ctx, l, SH2),
                  _layer_spec(w1, l), _layer_spec(w2, l), _mod_spec(d, has_ctx, l, GT2), _full_spec((1, d))],
        out_specs=_tok_spec(d),
        out_shape=jax.ShapeDtypeStruct((b, t, d), F32),
        compiler_params=_cparams(2), name="ffn",
    )(x, g, mods, mods, w1, w2, mods, g_final)


def kernel(x, c, ctx, c_ctx, w_ada, b_ada, g_norm_mix, g_norm_ffn, w_in_even, w_out_even, na_rpb, dn_conv, dn_a_log,
           dn_dt_bias, dn_g_out, w_in_odd, gm_g_v, gm_ws, gm_bs, w_out_odd, w_ff1, w_ff2, g_final):
    nb, seq, d = x.shape
    lc = ctx.shape[1]
    depth = w_ada.shape[0]
    assert nb <= CTX_ROW and seq % TM == 0 and lc == TM and seq % GRID_W == 0

    cond = jnp.zeros((MOD_ROWS, d), F32).at[:nb].set(c).at[CTX_ROW].set(c_ctx)
    mods = _adaln(cond, w_ada, b_ada)
    rope = _rope_tables(lc, seq)
    row = lambda a: a.reshape(1, -1).astype(F32)

    w_in_e = jnp.pad(w_in_even, ((0, 0), (0, 0), (0, LANES - 4 * DN_HEADS))).astype(BF16)
    w_out_e, w_in_o, w_out_o, ws_o = (w.astype(BF16) for w in (w_out_even, w_in_odd, w_out_odd, gm_ws))
    w1, w2 = w_ff1.astype(BF16), w_ff2.astype(BF16)

    xs = (ctx, x)
    has_ctx = True
    for l in range(depth):
        ctx_live = any(j % 2 == 0 for j in range(l + 1, depth))
        if l % 2 == 0:
            assert has_ctx
            e = l // 2
            qkv_a, qkv_b, gate, small = _even_in(xs, row(g_norm_mix[l]), mods, l, w_in_e, e)
            att = _na_attention(qkv_a, _na_bias_table(na_rpb[e]), lc, ctx_live)
            q, k, v, colsm, rowsm = _dn_prep(qkv_b, small, dn_conv[e], dn_a_log[e], dn_dt_bias[e], rope)
            o_f, o_b = _dn_chain(q, k, v, colsm, rowsm, lc)
            if isinstance(xs, tuple) and not ctx_live:
                xs = jnp.concatenate(xs, axis=1)
            xs = _even_out_ffn(xs, att, o_f, o_b, gate, row(dn_g_out[e]), w_out_e, e, row(g_norm_ffn[l]), mods,
                               w1, w2, l, row(g_final), ctx_live, l == depth - 1)
            has_ctx = ctx_live
        else:
            if isinstance(xs, tuple):
                xs = jnp.concatenate(xs, axis=1)
            o = l // 2
            bs = jnp.broadcast_to(gm_bs[o].astype(F32)[:, :, None], (GM_GROUPS, GM_CHUNK, LANES))
            xs = _gmlp(xs, row(g_norm_mix[l]), mods, l, w_in_o, row(gm_g_v[o]), ws_o, bs, w_out_o, o, has_ctx)
            xs = _ffn(xs, row(g_norm_ffn[l]), mods, w1, w2, l, row(g_final), has_ctx, l == depth - 1)
    return xs[:, lc:] if has_ctx else xs
```

```python
import functools

import jax
import jax.numpy as jnp
from jax import lax
from jax.experimental import pallas as pl
from jax.experimental.pallas import tpu as pltpu

EPS = 1e-6
GRID_W = 64
NA_HEADS, NA_DIM, NA_WIN_R, NA_WIN_C = 8, 64, 8, 16
A_W = NA_HEADS * NA_DIM
DN_HEADS, DN_DIM, DN_CHUNK = 4, 128, 64
DN_W = DN_HEADS * DN_DIM
ROPE_BASE = 10000.0
GM_GROUPS, GM_CHUNK = 8, 128

LANES = 128
TM = 256
DN_BATCH_UNROLL = 4
NA_ROWS_PER_STEP = 4
NA_CHAINS_IN_FLIGHT = 4
VMEM_LIMIT = 56 * 1024 * 1024
F32, BF16 = jnp.float32, jnp.bfloat16


def _cparams(n_axes):
    return pltpu.CompilerParams(dimension_semantics=("parallel",) * n_axes, vmem_limit_bytes=VMEM_LIMIT)


def _tok_spec(cols, off=0, col_block=0):
    return pl.BlockSpec((None, TM, cols), lambda b, j: (b, j + off, col_block))


MOD_ROWS = 16
CTX_ROW = 8
SH1, SC1, GT1, SH2, SC2, GT2 = range(6)


def _mod_spec(d, ctx_tile, l, kind):
    if ctx_tile:
        return pl.BlockSpec((None, None, None, 1, d), lambda b, j: (l, kind, jnp.where(j == 0, CTX_ROW, b), 0, 0))
    return pl.BlockSpec((None, None, None, 1, d), lambda b, j: (l, kind, b, 0, 0))


def _full_spec(shape):
    n = len(shape)
    return pl.BlockSpec(shape, lambda b, j: (0,) * n)


def _layer_spec(w, l):
    return pl.BlockSpec((None,) + w.shape[1:], lambda b, j: (l,) + (0,) * (w.ndim - 1), pipeline_mode=pl.Buffered(1))


def _stream_specs(xs, d, off=0):
    if isinstance(xs, tuple):
        return [pl.BlockSpec((None, TM, d), lambda b, j: (b, 0, 0)),
                pl.BlockSpec((None, TM, d), lambda b, j: (b, jnp.maximum(j - 1, 0), 0))]
    return [_tok_spec(d, off)]


def _stream_tile(refs):
    if len(refs) == 2:
        return jnp.where(pl.program_id(1) == 0, refs[0][...], refs[1][...])
    return refs[0][...]


def _stream_args(xs):
    return list(xs) if isinstance(xs, tuple) else [xs]


def _stream_shape(xs):
    if isinstance(xs, tuple):
        return xs[1].shape[0], xs[0].shape[1] + xs[1].shape[1], xs[1].shape[2]
    return xs.shape


def _norm_mod(x, g, sc, sh):
    y = x * lax.rsqrt(jnp.mean(x * x, axis=-1, keepdims=True) + EPS)
    return (y * g) * (1.0 + sc) + sh


def _silu(x):
    return x * jax.nn.sigmoid(x)


def _adaln_kernel(cond_ref, w_ref, b_ref, o_ref):
    a = _silu(cond_ref[...]).astype(BF16)
    m = jnp.dot(a, w_ref[...].astype(BF16), preferred_element_type=F32) + b_ref[...]
    for r in range(MOD_ROWS):
        o_ref[r] = m[r:r + 1, :]


def _adaln(cond, w_ada, b_ada):
    depth, d, n = w_ada.shape
    assert n == 6 * d
    return pl.pallas_call(
        _adaln_kernel,
        grid=(depth, 6),
        in_specs=[pl.BlockSpec((MOD_ROWS, d), lambda l, j: (0, 0)),
                  pl.BlockSpec((None, d, d), lambda l, j: (l, 0, j)),
                  pl.BlockSpec((None, 1, d), lambda l, j: (l, 0, j))],
        out_specs=pl.BlockSpec((None, None, MOD_ROWS, 1, d), lambda l, j: (l, j, 0, 0, 0)),
        out_shape=jax.ShapeDtypeStruct((depth, 6, MOD_ROWS, 1, d), F32),
        compiler_params=_cparams(2), name="adaln",
    )(cond, w_ada, b_ada.reshape(depth, 1, n))


def _even_in_kernel(*refs, n_x):
    g_ref, sc_ref, sh_ref, w_ref, qa_ref, qb_ref, gate_ref, sm_ref = refs[n_x:]
    h = _norm_mod(_stream_tile(refs[:n_x]), g_ref[...], sc_ref[...], sh_ref[...]).astype(BF16)
    p = jnp.dot(h, w_ref[...], preferred_element_type=F32)
    qa_ref[...] = p[:, :3 * A_W].astype(BF16)
    qb_ref[...] = p[:, 3 * A_W:3 * A_W + 3 * DN_W].astype(BF16)
    gate_ref[...] = p[:, 3 * A_W + 3 * DN_W:3 * A_W + 4 * DN_W].astype(BF16)
    sm_ref[...] = p[:, 3 * A_W + 4 * DN_W:]


def _even_in(xs, g, mods, l, w_pad, e):
    b, t, d = _stream_shape(xs)
    outs = [(3 * A_W, BF16), (3 * DN_W, BF16), (DN_W, BF16), (LANES, F32)]
    x_args = _stream_args(xs)
    return pl.pallas_call(
        functools.partial(_even_in_kernel, n_x=len(x_args)),
        grid=(b, t // TM),
        in_specs=_stream_specs(xs, d) + [_full_spec((1, d)), _mod_spec(d, True, l, SC1), _mod_spec(d, True, l, SH1),
                                         _layer_spec(w_pad, e)],
        out_specs=[_tok_spec(c) for c, _ in outs],
        out_shape=[jax.ShapeDtypeStruct((b, t, c), dt) for c, dt in outs],
        compiler_params=_cparams(2), name="even_in",
    )(*x_args, g, mods, mods, w_pad)


def _lockstep(gens):
    results = [None] * len(gens)
    live = list(range(len(gens)))
    while live:
        still = []
        for i in live:
            try:
                next(gens[i])
                still.append(i)
            except StopIteration as stop:
                results[i] = stop.value
        live = still
    return results


def _softmax_head_pair(q, keys, vals, biases):
    wq = q.shape[0]
    lane = lax.broadcasted_iota(jnp.int32, (2 * wq, LANES), 1)
    row = lax.broadcasted_iota(jnp.int32, (2 * wq, LANES), 0)
    own = (lane // NA_DIM) == (row // wq)
    q2 = jnp.concatenate([q, q], axis=0)
    qm = jnp.where(own, q2, jnp.zeros_like(q2))
    ss = [lax.dot_general(qm, kb, (((1,), (1,)), ((), ())), preferred_element_type=F32) for kb in keys]
    yield
    ss = [s if bias is None else s + bias for s, bias in zip(ss, biases)]
    m = functools.reduce(jnp.maximum, [jnp.max(s, axis=-1, keepdims=True) for s in ss])
    yield
    ps = [jnp.exp(s - m) for s in ss]
    den = functools.reduce(jnp.add, [jnp.sum(p, axis=-1, keepdims=True) for p in ps])
    yield
    o = functools.reduce(jnp.add, [jnp.dot(p.astype(BF16), vb, preferred_element_type=F32)
                                   for p, vb in zip(ps, vals)])
    yield
    o = o / den
    return jnp.where(lax.broadcasted_iota(jnp.int32, (wq, LANES), 1) < NA_DIM, o[:wq], o[wq:])


def _attend_jobs(q_ref, o_ref, jobs):
    scale = NA_DIM ** -0.5
    gens, dests = [], []
    for rsl, blocks, bias_of in jobs:
        for pair in range(NA_HEADS // 2):
            cs = slice(pair * LANES, (pair + 1) * LANES)
            keys, vals = blocks(cs)
            gens.append(_softmax_head_pair(q_ref[rsl, cs] * scale, keys, vals, bias_of(pair)))
            dests.append((rsl, cs))
    for g0 in range(0, len(gens), NA_CHAINS_IN_FLIGHT):
        outs = _lockstep(gens[g0:g0 + NA_CHAINS_IN_FLIGHT])
        for (rsl, cs), o in zip(dests[g0:g0 + NA_CHAINS_IN_FLIGHT], outs):
            o_ref[rsl, cs] = o.astype(o_ref.dtype)


def _na_kernel(q_ref, k_ref, v_ref, bias_ref, o_ref, *, lc, rows, ctx_steps):
    step = pl.program_id(1)
    n_loc = NA_WIN_R * GRID_W

    def latent_rows():
        jobs = []
        for i in range(NA_ROWS_PER_STEP):
            r = (step - ctx_steps) * NA_ROWS_PER_STEP + i
            rs = jnp.clip(r - NA_WIN_R // 2, 0, rows - NA_WIN_R)
            start = pl.multiple_of(lc + rs * GRID_W, GRID_W)
            blocks = lambda cs, start=start: ([k_ref[pl.ds(start, n_loc), cs], k_ref[0:lc, cs]],
                                              [v_ref[pl.ds(start, n_loc), cs], v_ref[0:lc, cs]])
            bias_of = lambda pair, var=r - rs: [bias_ref[var, 2 * pair:2 * pair + 2].reshape(2 * GRID_W, n_loc), None]
            jobs.append((slice(i * GRID_W, (i + 1) * GRID_W), blocks, bias_of))
        _attend_jobs(q_ref, o_ref, jobs)

    def context_block():
        blocks = lambda cs: ([k_ref[0:lc, cs]], [v_ref[0:lc, cs]])
        _attend_jobs(q_ref, o_ref, [(slice(None), blocks, lambda pair: [None])])

    if ctx_steps:
        pl.when(step < ctx_steps)(context_block)
        pl.when(step >= ctx_steps)(latent_rows)
    else:
        latent_rows()


def _na_bias_table(rpb):
    col = jnp.arange(GRID_W)
    cstart = jnp.clip(col - NA_WIN_C // 2, 0, GRID_W - NA_WIN_C)
    col_ok = (col[None, :] >= cstart[:, None]) & (col[None, :] < cstart[:, None] + NA_WIN_C)
    dc_idx = jnp.clip(col[None, :] - col[:, None], -(NA_WIN_C - 1), NA_WIN_C - 1) + NA_WIN_C - 1
    onehot = (dc_idx[:, :, None] == jnp.arange(2 * NA_WIN_C - 1)).astype(F32)
    v_i, w_i = jnp.arange(NA_WIN_R)[:, None], jnp.arange(NA_WIN_R)[None, :]
    rowhot = ((NA_WIN_R - 1 - v_i + w_i)[:, :, None] == jnp.arange(2 * NA_WIN_R - 1)).astype(F32)
    tab = jnp.einsum('hij,vwi,qkj->vhqwk', rpb.astype(F32), rowhot, onehot, precision=lax.Precision.HIGHEST)
    tab = jnp.where(col_ok[None, None, :, None, :], tab, -jnp.inf)
    return tab.reshape(NA_WIN_R, NA_HEADS, GRID_W, NA_WIN_R * GRID_W)


def _na_attention(qkv_a, bias_tab, lc, ctx_out):
    b, t, _ = qkv_a.shape
    rows = (t - lc) // GRID_W
    tq = NA_ROWS_PER_STEP * GRID_W
    assert rows >= NA_WIN_R and rows % NA_ROWS_PER_STEP == 0 and lc % tq == 0
    ctx_steps = lc // tq if ctx_out else 0
    q_off = 0 if ctx_out else lc // tq
    n_out = t if ctx_out else t - lc
    return pl.pallas_call(
        functools.partial(_na_kernel, lc=lc, rows=rows, ctx_steps=ctx_steps),
        grid=(b, ctx_steps + rows // NA_ROWS_PER_STEP),
        in_specs=[pl.BlockSpec((None, tq, A_W), lambda bi, s: (bi, s + q_off, 0)),
                  pl.BlockSpec((None, t, A_W), lambda bi, s: (bi, 0, 1)),
                  pl.BlockSpec((None, t, A_W), lambda bi, s: (bi, 0, 2)),
                  pl.BlockSpec(bias_tab.shape, lambda bi, s: (0, 0, 0, 0))],
        out_specs=pl.BlockSpec((None, tq, A_W), lambda bi, s: (bi, s, 0)),
        out_shape=jax.ShapeDtypeStruct((b, n_out, A_W), BF16),
        compiler_params=_cparams(2), name="na_attention",
    )(qkv_a, qkv_a, qkv_a, bias_tab)


def _chunk_scans(x):
    n = x.shape[0]
    pos = lax.broadcasted_iota(jnp.int32, x.shape, 0) % DN_CHUNK
    fwd, s = x, 1
    while s < DN_CHUNK:
        fwd = fwd + jnp.where(pos >= s, pltpu.roll(fwd, s, 0), 0.0)
        s *= 2
    grouped = fwd.reshape(n // DN_CHUNK, DN_CHUNK, x.shape[1])
    total = jnp.broadcast_to(grouped[:, DN_CHUNK - 1:DN_CHUNK, :], grouped.shape).reshape(x.shape)
    return fwd, total - fwd + x


def _conv_shift_matrix():
    r = lax.broadcasted_iota(jnp.int32, (2 * TM, TM + LANES), 0)
    c = lax.broadcasted_iota(jnp.int32, (2 * TM, TM + LANES), 1)
    down = (r < TM) & ((c == r - 1) | ((r == 0) & (c == TM + 15)))
    up = (r >= TM) & (((c == r - TM + 1) & (r < 2 * TM - 1)) | ((r == 2 * TM - 1) & (c == TM + 16)))
    return (down | up).astype(BF16)


def _dn_prep_kernel(x_ref, prev_ref, next_ref, sm_ref, shift_ref, cw_ref, cos_ref, sina_ref, sinb_ref, alog_ref,
                    dtb_ref, q_ref, k_ref, v_ref, col_ref, row_ref, *, n_tiles):
    j = pl.program_id(1)
    xb = x_ref[...]
    has_prev = jnp.logical_and(j != 0, j != 1)
    has_next = jnp.logical_and(j != 0, j != n_tiles - 1)
    halo_p = jnp.where(has_prev, prev_ref[...], jnp.zeros(prev_ref.shape, prev_ref.dtype))
    halo_n = jnp.where(has_next, next_ref[...], jnp.zeros(next_ref.shape, next_ref.dtype))
    ext = jnp.concatenate([xb, halo_p, halo_n, jnp.zeros((LANES - 32, xb.shape[1]), xb.dtype)], axis=0)
    shifted = jnp.dot(shift_ref[...], ext, preferred_element_type=F32)
    xm, xp = shifted[:TM], shifted[TM:]
    y = _silu(cw_ref[0:1, :] * xm + cw_ref[1:2, :] * xb.astype(F32) + cw_ref[2:3, :] * xp)

    cos, sina, sinb = cos_ref[...], sina_ref[...], sinb_ref[...]
    for h in range(DN_HEADS):
        for base, o_ref, scale in ((0, q_ref, DN_DIM ** -0.5), (DN_W, k_ref, None)):
            cs = slice(base + h * DN_DIM, base + (h + 1) * DN_DIM)
            z = y[:, cs]
            z = z * lax.rsqrt(jnp.sum(z * z, axis=-1, keepdims=True) + EPS)
            z = z * cos + pltpu.roll(z, 3 * DN_DIM // 4, 1) * sina + pltpu.roll(z, DN_DIM // 4, 1) * sinb
            if scale is not None:
                z = z * scale
            o_ref[:, h * DN_DIM:(h + 1) * DN_DIM] = z.astype(o_ref.dtype)
    v_ref[...] = y[:, 2 * DN_W:].astype(v_ref.dtype)

    sm = sm_ref[...]
    lane = lax.broadcasted_iota(jnp.int32, sm.shape, 1)
    logg = -jnp.exp(alog_ref[...]) * jax.nn.softplus(sm + dtb_ref[...])
    gam = jnp.where(lane < 3 * DN_HEADS, *_chunk_scans(logg))
    col = jnp.where(lane < 2 * DN_HEADS, jax.nn.sigmoid(sm), gam)
    col_ref[...] = col
    row_ref[...] = col.T[0:4 * DN_HEADS, :]


def _rope_tables(lc, seq):
    t = jnp.arange(seq)
    rowp = (t // GRID_W).astype(F32)
    colp = (t % GRID_W).astype(F32)
    n_freq = DN_DIM // 4
    inv = ROPE_BASE ** (-jnp.arange(n_freq, dtype=F32) / n_freq)
    ar, ac = rowp[:, None] * inv, colp[:, None] * inv
    ang = jnp.concatenate([ar, ar, ac, ac], axis=-1)
    cos, sin = jnp.cos(ang), jnp.sin(ang)
    quarter = (jnp.arange(DN_DIM) // n_freq) % 2
    sina = jnp.where(quarter == 0, -sin, 0.0)
    sinb = jnp.where(quarter == 1, sin, 0.0)
    pad = lambda a, v: jnp.concatenate([jnp.full((lc, DN_DIM), v, F32), a], axis=0)
    return pad(cos, 1.0), pad(sina, 0.0), pad(sinb, 0.0)


def _dn_prep(qkv_b, small, conv_w, a_log, dt_bias, rope):
    b, t, _ = qkv_b.shape
    n_tiles = t // TM
    hb = TM // 16
    pad16 = lambda a: jnp.zeros((1, LANES), F32).at[0, 2 * DN_HEADS:4 * DN_HEADS].set(a.reshape(-1).astype(F32))
    outs = [(DN_W, BF16), (DN_W, BF16), (DN_W, BF16), (LANES, F32)]
    res = pl.pallas_call(
        functools.partial(_dn_prep_kernel, n_tiles=n_tiles),
        grid=(b, n_tiles),
        in_specs=[_tok_spec(3 * DN_W),
                  pl.BlockSpec((None, 16, 3 * DN_W), lambda bi, j: (bi, jnp.maximum(j * hb - 1, 0), 0)),
                  pl.BlockSpec((None, 16, 3 * DN_W), lambda bi, j: (bi, jnp.minimum((j + 1) * hb, t // 16 - 1), 0)),
                  _tok_spec(LANES),
                  _full_spec((2 * TM, TM + LANES)),
                  _full_spec((3, 3 * DN_W)),
                  pl.BlockSpec((TM, DN_DIM), lambda bi, j: (j, 0)),
                  pl.BlockSpec((TM, DN_DIM), lambda bi, j: (j, 0)),
                  pl.BlockSpec((TM, DN_DIM), lambda bi, j: (j, 0)),
                  _full_spec((1, LANES)), _full_spec((1, LANES))],
        out_specs=[_tok_spec(c) for c, _ in outs] + [pl.BlockSpec((None, 4 * DN_HEADS, TM), lambda bi, j: (bi, 0, j))],
        out_shape=[jax.ShapeDtypeStruct((b, t, c), dt) for c, dt in outs]
        + [jax.ShapeDtypeStruct((b, 4 * DN_HEADS, t), F32)],
        compiler_params=_cparams(2), name="dn_prep",
    )(qkv_b, qkv_b, qkv_b, small, _conv_shift_matrix(), conv_w.astype(F32), *rope, pad16(a_log), pad16(dt_bias))
    q, k, v, colsm, rowsm = res
    n_chunks = t // DN_CHUNK
    rowsm = rowsm.reshape(b, 4 * DN_HEADS, n_chunks, DN_CHUNK).transpose(2, 0, 1, 3)
    rowsm = rowsm.reshape(n_chunks, b, 2 * DN_HEADS, 2 * DN_CHUNK)
    return q, k, v, colsm, rowsm


def _block_diag(x):
    n, w = x.shape
    x2 = jnp.concatenate([x, x], axis=0)
    row = lax.broadcasted_iota(jnp.int32, x2.shape, 0)
    lane = lax.broadcasted_iota(jnp.int32, x2.shape, 1)
    return jnp.where((row // n) == (lane // (w // 2)), x2, jnp.zeros_like(x2))


def _dn_pair_step(d, p, q_all, k_all, v_all, cm, rm, s_prev):
    c = DN_CHUNK
    heads = (2 * p, 2 * p + 1)
    cs2 = slice(2 * p * DN_DIM, (2 * p + 2) * DN_DIM)
    q2, k2, v2 = q_all[:, cs2], k_all[:, cs2], v_all[:, cs2]
    lane = lax.broadcasted_iota(jnp.int32, (c, 2 * c), 1)
    ri = lax.broadcasted_iota(jnp.int32, (c, 2 * c), 0)
    ci = lane % c
    first = lane < c
    col = lambda base: [cm[:, base + h:base + h + 1] for h in heads]
    beta, gam_c = col(d * DN_HEADS), col(2 * DN_HEADS + d * DN_HEADS)
    side = lambda ab: jnp.where(first, ab[0], ab[1])
    ir = (2 * DN_HEADS + d * DN_HEADS) // 2 + p
    gam_r = rm[ir:ir + 1, :]
    if d == 0:
        incl, strict, g_last = ri >= ci, ri > ci, [g[c - 1:c, :] for g in gam_c]
    else:
        incl, strict, g_last = ri <= ci, ri < ci, [g[0:1, :] for g in gam_c]
    decay = jnp.exp(jnp.where(incl, side(gam_c) - gam_r, -jnp.inf))
    zk = jnp.zeros((c, DN_DIM), BF16)
    kd = jnp.concatenate([jnp.concatenate([k2[:, :DN_DIM], zk], axis=1),
                          jnp.concatenate([zk, k2[:, DN_DIM:]], axis=1)], axis=0)
    g = lax.dot_general(jnp.concatenate([k2, q2], axis=0), kd, (((1,), (1,)), ((), ())),
                        preferred_element_type=F32)
    yield
    kk, qk = g[:c], g[c:]
    lneg = jnp.where(strict, -(side(beta) * kk * decay), 0.0)
    aqk = qk * decay
    eg = [jnp.exp(gc) for gc in gam_c]
    kf = [k2[:, i * DN_DIM:(i + 1) * DN_DIM].astype(F32) for i in range(2)]
    vf = [v2[:, i * DN_DIM:(i + 1) * DN_DIM].astype(F32) for i in range(2)]
    qf = [q2[:, i * DN_DIM:(i + 1) * DN_DIM].astype(F32) for i in range(2)]
    rhs = [jnp.concatenate([(beta[i] * eg[i]) * kf[i], beta[i] * vf[i]], axis=1).astype(BF16) for i in range(2)]
    eye = (ri == ci).astype(F32)
    same_blk = lambda s: (ri // s) == (ci // s)
    dneg = jnp.where(same_blk(4), lneg, 0.0)
    dnb = dneg.astype(BF16)
    dsq = jnp.dot(dnb, _block_diag(dnb), preferred_element_type=F32)
    yield
    t = (eye + dneg) + jnp.dot((eye + dneg).astype(BF16), _block_diag(dsq.astype(BF16)), preferred_element_type=F32)
    yield
    s = 4
    while s < c:
        eneg = jnp.where(jnp.logical_and(same_blk(2 * s), jnp.logical_not(same_blk(s))), lneg, 0.0)
        tb = t.astype(BF16)
        te = jnp.dot(tb, _block_diag(eneg.astype(BF16)), preferred_element_type=F32)
        yield
        t = t + jnp.dot(te.astype(BF16), _block_diag(tb), preferred_element_type=F32)
        yield
        s *= 2
    sol = jnp.dot(_block_diag(t.astype(BF16)), jnp.concatenate(rhs, axis=0), preferred_element_type=F32).astype(BF16)
    yield
    kdec = jnp.concatenate([kf[i] * jnp.exp(g_last[i] - gam_c[i]) for i in range(2)], axis=0)
    lhs = jnp.concatenate([_block_diag(aqk.astype(BF16)), _block_diag(kdec.T.astype(BF16))], axis=0)
    res = jnp.dot(lhs, sol, preferred_element_type=F32)
    yield
    outs, states = [], []
    for i in range(2):
        qo = res[i * c:(i + 1) * c]
        ab = res[2 * c + i * DN_DIM:2 * c + (i + 1) * DN_DIM]
        qeff = qf[i] * eg[i] - qo[:, :DN_DIM]
        both = jnp.dot(jnp.concatenate([qeff, ab[:, :DN_DIM]], axis=0).astype(BF16), s_prev[i].astype(BF16),
                       preferred_element_type=F32)
        outs.append(both[:c] + qo[:, DN_DIM:])
        states.append(jnp.exp(g_last[i]) * s_prev[i] - both[c:] + ab[:, DN_DIM:])
    return outs, states


def _dn_chain_kernel(qf_ref, kf_ref, vf_ref, cf_ref, rf_ref, qb_ref, kb_ref, vb_ref, cb_ref, rb_ref,
                     of_ref, ob_ref, s_ref, *, nb):
    @pl.when(pl.program_id(0) == 0)
    def _():
        s_ref[...] = jnp.zeros_like(s_ref)

    in_refs = ((qf_ref, kf_ref, vf_ref, cf_ref, rf_ref), (qb_ref, kb_ref, vb_ref, cb_ref, rb_ref))
    per_iter = DN_BATCH_UNROLL if nb % DN_BATCH_UNROLL == 0 else 1

    def body(i, carry):
        bs = [i * per_iter + u for u in range(per_iter)]
        n_pairs = DN_HEADS // 2
        gens = [_dn_pair_step(d, p, *[r[b] for r in in_refs[d]], [s_ref[d, b, 2 * p], s_ref[d, b, 2 * p + 1]])
                for b in bs for d in range(2) for p in range(n_pairs)]
        res = _lockstep(gens)
        for ib, b in enumerate(bs):
            for d, o_r in enumerate((of_ref, ob_ref)):
                base = (ib * 2 + d) * n_pairs
                o_r[b] = jnp.concatenate([o for p in range(n_pairs) for o in res[base + p][0]], axis=1)
                for p in range(n_pairs):
                    for i in range(2):
                        s_ref[d, b, 2 * p + i] = res[base + p][1][i]
        return carry

    lax.fori_loop(0, nb // per_iter, body, 0)


def _dn_chain(q, k, v, colsm, rowsm, lc):
    b, t, _ = q.shape
    n_chunks = t // DN_CHUNK
    nc_ctx = lc // DN_CHUNK
    fwd = lambda i: i
    bwd = lambda i: jnp.where(i < nc_ctx, nc_ctx - 1 - i, n_chunks - 1 + nc_ctx - i)
    tok = lambda cols, cm: pl.BlockSpec((b, DN_CHUNK, cols), lambda i: (0, cm(i), 0))
    rowspec = lambda cm: pl.BlockSpec((None, b, 2 * DN_HEADS, 2 * DN_CHUNK), lambda i: (cm(i), 0, 0, 0))
    in_specs = []
    for cm in (fwd, bwd):
        in_specs += [tok(DN_W, cm), tok(DN_W, cm), tok(DN_W, cm), tok(LANES, cm), rowspec(cm)]
    return pl.pallas_call(
        functools.partial(_dn_chain_kernel, nb=b),
        grid=(n_chunks,),
        in_specs=in_specs,
        out_specs=[tok(DN_W, fwd), tok(DN_W, bwd)],
        out_shape=[jax.ShapeDtypeStruct((b, t, DN_W), F32)] * 2,
        scratch_shapes=[pltpu.VMEM((2, b, DN_HEADS, DN_DIM, DN_DIM), F32)],
        compiler_params=pltpu.CompilerParams(dimension_semantics=("arbitrary",), vmem_limit_bytes=VMEM_LIMIT),
        name="dn_chain",
    )(q, k, v, colsm, rowsm, q, k, v, colsm, rowsm)


def _ffn_math(x, g, sc, sh, w1_ref, w2_ref, gt, gf, final_norm):
    h = _norm_mod(x, g, sc, sh).astype(BF16)
    a = jnp.maximum(jnp.dot(h, w1_ref[...], preferred_element_type=F32), 0.0)
    y = jnp.dot((a * a).astype(BF16), w2_ref[...], preferred_element_type=F32)
    out = x + gt * y
    if final_norm:
        out = out * lax.rsqrt(jnp.mean(out * out, axis=-1, keepdims=True) + EPS) * gf
    return out


def _even_out_ffn_kernel(*refs, n_x, final_norm):
    (att_ref, of_ref, ob_ref, gate_ref, gout_ref, w_ref, gt1_ref,
     g_ref, sc_ref, sh_ref, w1_ref, w2_ref, gt2_ref, gf_ref, o_ref) = refs[n_x:]
    o = of_ref[...] + ob_ref[...]
    gate = gate_ref[...].astype(F32)
    parts = [att_ref[...]]
    for h in range(DN_HEADS):
        cs = slice(h * DN_DIM, (h + 1) * DN_DIM)
        oh = o[:, cs]
        yh = oh * lax.rsqrt(jnp.mean(oh * oh, axis=-1, keepdims=True) + EPS) * gout_ref[...]
        parts.append((yh * _silu(gate[:, cs])).astype(BF16))
    a = jnp.concatenate(parts, axis=1)
    y = jnp.dot(a, w_ref[...], preferred_element_type=F32)
    x1 = _stream_tile(refs[:n_x]) + gt1_ref[...] * y
    o_ref[...] = _ffn_math(x1, g_ref[...], sc_ref[...], sh_ref[...], w1_ref, w2_ref, gt2_ref[...], gf_ref[...],
                           final_norm)


def _even_out_ffn(xs, att, o_f, o_b, gate, g_out, w_out, e, g2, mods, w1, w2, l, g_final, ctx_out, final_norm):
    b, t, d = _stream_shape(xs)
    off = 0 if ctx_out else 1
    assert ctx_out or not isinstance(xs, tuple)
    n_t = t // TM - off
    mod = lambda kind: _mod_spec(d, ctx_out, l, kind)
    x_args = _stream_args(xs)
    return pl.pallas_call(
        functools.partial(_even_out_ffn_kernel, n_x=len(x_args), final_norm=final_norm),
        grid=(b, n_t),
        in_specs=_stream_specs(xs, d, off) + [
            _tok_spec(A_W), _tok_spec(DN_W, off), _tok_spec(DN_W, off), _tok_spec(DN_W, off),
            _full_spec((1, DN_DIM)), _layer_spec(w_out, e), mod(GT1),
            _full_spec((1, d)), mod(SC2), mod(SH2), _layer_spec(w1, l), _layer_spec(w2, l), mod(GT2),
            _full_spec((1, d))],
        out_specs=_tok_spec(d),
        out_shape=jax.ShapeDtypeStruct((b, n_t * TM, d), F32),
        compiler_params=_cparams(2), name="even_out_ffn",
    )(*x_args, att, o_f, o_b, gate, g_out, w_out, mods, g2, mods, mods, w1, w2, mods, g_final)


def _gelu(x):
    return 0.5 * x * (1.0 + lax.erf(x * (2.0 ** -0.5)))


def _skewed(gens):
    results = [None] * len(gens)
    live, started = [], 0
    while started < len(gens) or live:
        if started < len(gens):
            live.append(started)
            started += 1
        still = []
        for i in reversed(live):
            try:
                next(gens[i])
                still.append(i)
            except StopIteration as stop:
                results[i] = stop.value
        live = still[::-1]
    return results


def _gmlp_ffn_kernel(x_ref, g_ref, sc_ref, sh_ref, wi_ref, gv_ref, ws_ref, bs_ref, wo_ref, gt_ref,
                     g2_ref, sc2_ref, sh2_ref, w1_ref, w2_ref, gt2_ref, gf_ref, o_ref, u_ref, v_ref, *, final_norm):
    x = x_ref[...]
    h = _norm_mod(x, g_ref[...], sc_ref[...], sh_ref[...]).astype(BF16)
    half = wi_ref.shape[1] // 2
    gw = half // GM_GROUPS

    pw = 2 * gw
    n_pairs = GM_GROUPS // 2

    def project(c0, dst_ref, want_ssq):
        z = jnp.dot(h, wi_ref[:, c0:c0 + pw], preferred_element_type=F32)
        yield
        z = _gelu(z)
        d0 = c0 % half
        dst_ref[:, d0:d0 + pw] = z.astype(dst_ref.dtype)
        return jnp.sum(z * z, axis=-1, keepdims=True) if want_ssq else None

    cols = [(half + p * pw, v_ref, True) for p in range(n_pairs)] + [(p * pw, u_ref, False) for p in range(n_pairs)]
    ssq = functools.reduce(jnp.add, _skewed([project(*c) for c in cols])[:n_pairs])
    rinv = lax.rsqrt(ssq * (1.0 / half) + EPS)

    def mix(p):
        cs = slice(p * pw, (p + 1) * pw)
        vb = (v_ref[:, cs] * rinv * gv_ref[:, cs]).astype(BF16)
        mixed = [[jnp.dot(ws_ref[2 * p + k], vb[n * GM_CHUNK:(n + 1) * GM_CHUNK, k * gw:(k + 1) * gw],
                          preferred_element_type=F32) for k in range(2)] for n in range(TM // GM_CHUNK)]
        yield
        bias = [jnp.concatenate([bs_ref[2 * p + k]] * (gw // LANES), axis=1) for k in range(2)]
        t = jnp.concatenate([jnp.concatenate([row[k] + bias[k] for k in range(2)], axis=1) for row in mixed], axis=0)
        t = (t * u_ref[:, cs]).astype(BF16)
        return jnp.dot(t, wo_ref[cs, :], preferred_element_type=F32)

    y = functools.reduce(jnp.add, _skewed([mix(p) for p in range(n_pairs)]))
    x1 = x + gt_ref[...] * y
    o_ref[...] = _ffn_math(x1, g2_ref[...], sc2_ref[...], sh2_ref[...], w1_ref, w2_ref, gt2_ref[...], gf_ref[...],
                           final_norm)


def _gmlp_ffn(x, g, mods, l, w_in, g_v, ws, bs, w_out, o, g2, w1, w2, g_final, has_ctx, final_norm):
    b, t, d = x.shape
    half = w_in.shape[2] // 2
    assert (half // GM_GROUPS) % LANES == 0
    mod = lambda kind: _mod_spec(d, has_ctx, l, kind)
    return pl.pallas_call(
        functools.partial(_gmlp_ffn_kernel, final_norm=final_norm),
        grid=(b, t // TM),
        in_specs=[_tok_spec(d), _full_spec((1, d)), mod(SC1), mod(SH1),
                  _layer_spec(w_in, o), _full_spec((1, half)), _layer_spec(ws, o), _full_spec(bs.shape),
                  _layer_spec(w_out, o), mod(GT1),
                  _full_spec((1, d)), mod(SC2), mod(SH2), _layer_spec(w1, l), _layer_spec(w2, l), mod(GT2),
                  _full_spec((1, d))],
        out_specs=_tok_spec(d),
        out_shape=jax.ShapeDtypeStruct((b, t, d), F32),
        scratch_shapes=[pltpu.VMEM((TM, half), F32), pltpu.VMEM((TM, half), F32)],
        compiler_params=_cparams(2), name="gmlp_ffn",
    )(x, g, mods, mods, w_in, g_v, ws, bs, w_out, mods, g2, mods, mods, w1, w2, mods, g_final)


def kernel(x, c, ctx, c_ctx, w_ada, b_ada, g_norm_mix, g_norm_ffn, w_in_even, w_out_even, na_rpb, dn_conv, dn_a_log,
           dn_dt_bias, dn_g_out, w_in_odd, gm_g_v, gm_ws, gm_bs, w_out_odd, w_ff1, w_ff2, g_final):
    nb, seq, d = x.shape
    lc = ctx.shape[1]
    depth = w_ada.shape[0]
    assert nb <= CTX_ROW and seq % TM == 0 and lc == TM and seq % GRID_W == 0

    cond = jnp.zeros((MOD_ROWS, d), F32).at[:nb].set(c).at[CTX_ROW].set(c_ctx)
    mods = _adaln(cond, w_ada, b_ada)
    rope = _rope_tables(lc, seq)
    row = lambda a: a.reshape(1, -1).astype(F32)

    w_in_e = jnp.pad(w_in_even, ((0, 0), (0, 0), (0, LANES - 4 * DN_HEADS))).astype(BF16)
    w_out_e, w_in_o, w_out_o, ws_o = (w.astype(BF16) for w in (w_out_even, w_in_odd, w_out_odd, gm_ws))
    w1, w2 = w_ff1.astype(BF16), w_ff2.astype(BF16)

    xs = (ctx, x)
    has_ctx = True
    for l in range(depth):
        ctx_live = any(j % 2 == 0 for j in range(l + 1, depth))
        if l % 2 == 0:
            assert has_ctx
            e = l // 2
            qkv_a, qkv_b, gate, small = _even_in(xs, row(g_norm_mix[l]), mods, l, w_in_e, e)
            att = _na_attention(qkv_a, _na_bias_table(na_rpb[e]), lc, ctx_live)
            q, k, v, colsm, rowsm = _dn_prep(qkv_b, small, dn_conv[e], dn_a_log[e], dn_dt_bias[e], rope)
            o_f, o_b = _dn_chain(q, k, v, colsm, rowsm, lc)
            if isinstance(xs, tuple) and not ctx_live:
                xs = jnp.concatenate(xs, axis=1)
            xs = _even_out_ffn(xs, att, o_f, o_b, gate, row(dn_g_out[e]), w_out_e, e, row(g_norm_ffn[l]), mods,
                               w1, w2, l, row(g_final), ctx_live, l == depth - 1)
            has_ctx = ctx_live
        else:
            if isinstance(xs, tuple):
                xs = jnp.concatenate(xs, axis=1)
            o = l // 2
            bs = jnp.broadcast_to(gm_bs[o].astype(F32)[:, :, None], (GM_GROUPS, GM_CHUNK, LANES))
            xs = _gmlp_ffn(xs, row(g_norm_mix[l]), mods, l, w_in_o, row(gm_g_v[o]), ws_o, bs, w_out_o, o,
                           row(g_norm_ffn[l]), w1, w2, row(g_final), has_ctx, l == depth - 1)
    return xs[:, lc:] if has_ctx else xs
```

```python
import functools

import jax
import jax.numpy as jnp
from jax import lax
from jax.experimental import pallas as pl
from jax.experimental.pallas import tpu as pltpu

EPS = 1e-6
GRID_W = 64
NA_HEADS, NA_DIM, NA_WIN_R, NA_WIN_C = 8, 64, 8, 16
A_W = NA_HEADS * NA_DIM
DN_HEADS, DN_DIM, DN_CHUNK = 4, 128, 64
DN_W = DN_HEADS * DN_DIM
ROPE_BASE = 10000.0
GM_GROUPS, GM_CHUNK = 8, 128

LANES = 128
TM = 256
DN_BATCH_UNROLL = 8
NA_ROWS_PER_STEP = 4
NA_CHAINS_IN_FLIGHT = 4
VMEM_LIMIT = 56 * 1024 * 1024
MOD_ROWS = 16
CTX_ROW = 8
SH1, SC1, GT1, SH2, SC2, GT2 = range(6)
F32, BF16 = jnp.float32, jnp.bfloat16


def _cparams(n_axes):
    return pltpu.CompilerParams(dimension_semantics=("parallel",) * n_axes, vmem_limit_bytes=VMEM_LIMIT)


def _tok_spec(cols, off=0):
    return pl.BlockSpec((None, TM, cols), lambda b, j: (b, j + off, 0))


def _mod_spec(d, ctx_tile, l, kind):
    if ctx_tile:
        return pl.BlockSpec((None, None, None, 1, d), lambda b, j: (l, kind, jnp.where(j == 0, CTX_ROW, b), 0, 0))
    return pl.BlockSpec((None, None, None, 1, d), lambda b, j: (l, kind, b, 0, 0))


def _full_spec(shape):
    n = len(shape)
    return pl.BlockSpec(shape, lambda b, j: (0,) * n)


def _layer_spec(w, l):
    return pl.BlockSpec((None,) + w.shape[1:], lambda b, j: (l,) + (0,) * (w.ndim - 1), pipeline_mode=pl.Buffered(1))


def _stream_specs(xs, d, off=0):
    if isinstance(xs, tuple):
        return [pl.BlockSpec((None, TM, d), lambda b, j: (b, 0, 0)),
                pl.BlockSpec((None, TM, d), lambda b, j: (b, jnp.maximum(j - 1, 0), 0))]
    return [_tok_spec(d, off)]


def _stream_tile(refs):
    if len(refs) == 2:
        return jnp.where(pl.program_id(1) == 0, refs[0][...], refs[1][...])
    return refs[0][...]


def _stream_args(xs):
    return list(xs) if isinstance(xs, tuple) else [xs]


def _stream_shape(xs):
    if isinstance(xs, tuple):
        return xs[1].shape[0], xs[0].shape[1] + xs[1].shape[1], xs[1].shape[2]
    return xs.shape


def _norm_mod(x, g, sc, sh):
    y = x * lax.rsqrt(jnp.mean(x * x, axis=-1, keepdims=True) + EPS)
    return (y * g) * (1.0 + sc) + sh


def _silu(x):
    return x * jax.nn.sigmoid(x)


def _adaln_kernel(cond_ref, w_ref, b_ref, o_ref):
    a = _silu(cond_ref[...]).astype(BF16)
    m = jnp.dot(a, w_ref[...].astype(BF16), preferred_element_type=F32) + b_ref[...]
    for r in range(MOD_ROWS):
        o_ref[r] = m[r:r + 1, :]


def _adaln(cond, w_ada, b_ada):
    depth, d, n = w_ada.shape
    assert n == 6 * d
    return pl.pallas_call(
        _adaln_kernel,
        grid=(depth, 6),
        in_specs=[pl.BlockSpec((MOD_ROWS, d), lambda l, j: (0, 0)),
                  pl.BlockSpec((None, d, d), lambda l, j: (l, 0, j)),
                  pl.BlockSpec((None, 1, d), lambda l, j: (l, 0, j))],
        out_specs=pl.BlockSpec((None, None, MOD_ROWS, 1, d), lambda l, j: (l, j, 0, 0, 0)),
        out_shape=jax.ShapeDtypeStruct((depth, 6, MOD_ROWS, 1, d), F32),
        compiler_params=_cparams(2), name="adaln",
    )(cond, w_ada, b_ada.reshape(depth, 1, n))


def _even_in_kernel(*refs, n_x):
    g_ref, sc_ref, sh_ref, w_ref, qa_ref, qb_ref, gate_ref, sm_ref = refs[n_x:]
    h = _norm_mod(_stream_tile(refs[:n_x]), g_ref[...], sc_ref[...], sh_ref[...]).astype(BF16)
    p = jnp.dot(h, w_ref[...], preferred_element_type=F32)
    qa_ref[...] = p[:, :3 * A_W].astype(BF16)
    qb_ref[...] = p[:, 3 * A_W:3 * A_W + 3 * DN_W].astype(BF16)
    gate_ref[...] = p[:, 3 * A_W + 3 * DN_W:3 * A_W + 4 * DN_W].astype(BF16)
    sm_ref[...] = p[:, 3 * A_W + 4 * DN_W:]


def _even_in(xs, g, mods, l, w_pad, e):
    b, t, d = _stream_shape(xs)
    outs = [(3 * A_W, BF16), (3 * DN_W, BF16), (DN_W, BF16), (LANES, F32)]
    x_args = _stream_args(xs)
    return pl.pallas_call(
        functools.partial(_even_in_kernel, n_x=len(x_args)),
        grid=(b, t // TM),
        in_specs=_stream_specs(xs, d) + [_full_spec((1, d)), _mod_spec(d, True, l, SC1), _mod_spec(d, True, l, SH1),
                                         _layer_spec(w_pad, e)],
        out_specs=[_tok_spec(c) for c, _ in outs],
        out_shape=[jax.ShapeDtypeStruct((b, t, c), dt) for c, dt in outs],
        compiler_params=_cparams(2), name="even_in",
    )(*x_args, g, mods, mods, w_pad)


def _lockstep(gens):
    results = [None] * len(gens)
    live = list(range(len(gens)))
    while live:
        still = []
        for i in live:
            try:
                next(gens[i])
                still.append(i)
            except StopIteration as stop:
                results[i] = stop.value
        live = still
    return results


def _softmax_head_pair(q, keys, vals, biases):
    wq = q.shape[0]
    lane = lax.broadcasted_iota(jnp.int32, (2 * wq, LANES), 1)
    row = lax.broadcasted_iota(jnp.int32, (2 * wq, LANES), 0)
    own = (lane // NA_DIM) == (row // wq)
    q2 = jnp.concatenate([q, q], axis=0)
    qm = jnp.where(own, q2, jnp.zeros_like(q2))
    ss = [lax.dot_general(qm, kb, (((1,), (1,)), ((), ())), preferred_element_type=F32) for kb in keys]
    yield
    ss = [s if bias is None else s + bias for s, bias in zip(ss, biases)]
    m = functools.reduce(jnp.maximum, [jnp.max(s, axis=-1, keepdims=True) for s in ss])
    yield
    ps = [jnp.exp(s - m) for s in ss]
    den = functools.reduce(jnp.add, [jnp.sum(p, axis=-1, keepdims=True) for p in ps])
    yield
    o = functools.reduce(jnp.add, [jnp.dot(p.astype(BF16), vb, preferred_element_type=F32)
                                   for p, vb in zip(ps, vals)])
    yield
    o = o / den
    return jnp.where(lax.broadcasted_iota(jnp.int32, (wq, LANES), 1) < NA_DIM, o[:wq], o[wq:])


def _attend_jobs(q_ref, o_ref, jobs):
    scale = NA_DIM ** -0.5
    gens, dests = [], []
    for rsl, blocks, bias_of in jobs:
        for pair in range(NA_HEADS // 2):
            cs = slice(pair * LANES, (pair + 1) * LANES)
            keys, vals = blocks(cs)
            gens.append(_softmax_head_pair(q_ref[rsl, cs] * scale, keys, vals, bias_of(pair)))
            dests.append((rsl, cs))
    for g0 in range(0, len(gens), NA_CHAINS_IN_FLIGHT):
        outs = _lockstep(gens[g0:g0 + NA_CHAINS_IN_FLIGHT])
        for (rsl, cs), o in zip(dests[g0:g0 + NA_CHAINS_IN_FLIGHT], outs):
            o_ref[rsl, cs] = o.astype(o_ref.dtype)


def _na_kernel(q_ref, k_ref, v_ref, bias_ref, o_ref, *, lc, rows, ctx_steps):
    step = pl.program_id(1)
    n_loc = NA_WIN_R * GRID_W

    def latent_rows():
        jobs = []
        for i in range(NA_ROWS_PER_STEP):
            r = (step - ctx_steps) * NA_ROWS_PER_STEP + i
            rs = jnp.clip(r - NA_WIN_R // 2, 0, rows - NA_WIN_R)
            start = pl.multiple_of(lc + rs * GRID_W, GRID_W)
            blocks = lambda cs, start=start: ([k_ref[pl.ds(start, n_loc), cs], k_ref[0:lc, cs]],
                                              [v_ref[pl.ds(start, n_loc), cs], v_ref[0:lc, cs]])
            bias_of = lambda pair, var=r - rs: [bias_ref[var, 2 * pair:2 * pair + 2].reshape(2 * GRID_W, n_loc), None]
            jobs.append((slice(i * GRID_W, (i + 1) * GRID_W), blocks, bias_of))
        _attend_jobs(q_ref, o_ref, jobs)

    def context_block():
        blocks = lambda cs: ([k_ref[0:lc, cs]], [v_ref[0:lc, cs]])
        _attend_jobs(q_ref, o_ref, [(slice(None), blocks, lambda pair: [None])])

    if ctx_steps:
        pl.when(step < ctx_steps)(context_block)
        pl.when(step >= ctx_steps)(latent_rows)
    else:
        latent_rows()


def _na_bias_table(rpb):
    col = jnp.arange(GRID_W)
    cstart = jnp.clip(col - NA_WIN_C // 2, 0, GRID_W - NA_WIN_C)
    col_ok = (col[None, :] >= cstart[:, None]) & (col[None, :] < cstart[:, None] + NA_WIN_C)
    dc_idx = jnp.clip(col[None, :] - col[:, None], -(NA_WIN_C - 1), NA_WIN_C - 1) + NA_WIN_C - 1
    onehot = (dc_idx[:, :, None] == jnp.arange(2 * NA_WIN_C - 1)).astype(F32)
    v_i, w_i = jnp.arange(NA_WIN_R)[:, None], jnp.arange(NA_WIN_R)[None, :]
    rowhot = ((NA_WIN_R - 1 - v_i + w_i)[:, :, None] == jnp.arange(2 * NA_WIN_R - 1)).astype(F32)
    tab = jnp.einsum('hij,vwi,qkj->vhqwk', rpb.astype(F32), rowhot, onehot, precision=lax.Precision.HIGHEST)
    tab = jnp.where(col_ok[None, None, :, None, :], tab, -jnp.inf)
    return tab.reshape(NA_WIN_R, NA_HEADS, GRID_W, NA_WIN_R * GRID_W)


def _na_attention(qkv_a, bias_tab, lc, ctx_out):
    b, t, _ = qkv_a.shape
    rows = (t - lc) // GRID_W
    tq = NA_ROWS_PER_STEP * GRID_W
    assert rows >= NA_WIN_R and rows % NA_ROWS_PER_STEP == 0 and lc % tq == 0
    ctx_steps = lc // tq if ctx_out else 0
    q_off = 0 if ctx_out else lc // tq
    n_out = t if ctx_out else t - lc
    return pl.pallas_call(
        functools.partial(_na_kernel, lc=lc, rows=rows, ctx_steps=ctx_steps),
        grid=(b, ctx_steps + rows // NA_ROWS_PER_STEP),
        in_specs=[pl.BlockSpec((None, tq, A_W), lambda bi, s: (bi, s + q_off, 0)),
                  pl.BlockSpec((None, t, A_W), lambda bi, s: (bi, 0, 1)),
                  pl.BlockSpec((None, t, A_W), lambda bi, s: (bi, 0, 2)),
                  pl.BlockSpec(bias_tab.shape, lambda bi, s: (0, 0, 0, 0))],
        out_specs=pl.BlockSpec((None, tq, A_W), lambda bi, s: (bi, s, 0)),
        out_shape=jax.ShapeDtypeStruct((b, n_out, A_W), BF16),
        compiler_params=_cparams(2), name="na_attention",
    )(qkv_a, qkv_a, qkv_a, bias_tab)


def _chunk_scans(x):
    n = x.shape[0]
    pos = lax.broadcasted_iota(jnp.int32, x.shape, 0) % DN_CHUNK
    fwd, s = x, 1
    while s < DN_CHUNK:
        fwd = fwd + jnp.where(pos >= s, pltpu.roll(fwd, s, 0), 0.0)
        s *= 2
    grouped = fwd.reshape(n // DN_CHUNK, DN_CHUNK, x.shape[1])
    total = jnp.broadcast_to(grouped[:, DN_CHUNK - 1:DN_CHUNK, :], grouped.shape).reshape(x.shape)
    return fwd, total - fwd + x


def _conv_shift_matrix():
    r = lax.broadcasted_iota(jnp.int32, (2 * TM, TM + LANES), 0)
    c = lax.broadcasted_iota(jnp.int32, (2 * TM, TM + LANES), 1)
    down = (r < TM) & ((c == r - 1) | ((r == 0) & (c == TM + 15)))
    up = (r >= TM) & (((c == r - TM + 1) & (r < 2 * TM - 1)) | ((r == 2 * TM - 1) & (c == TM + 16)))
    return (down | up).astype(BF16)


def _dn_prep_kernel(x_ref, prev_ref, next_ref, sm_ref, shift_ref, cw_ref, cos_ref, sina_ref, sinb_ref, alog_ref,
                    dtb_ref, q_ref, k_ref, v_ref, col_ref, row_ref, *, n_tiles):
    j = pl.program_id(1)
    xb = x_ref[...]
    has_prev = jnp.logical_and(j != 0, j != 1)
    has_next = jnp.logical_and(j != 0, j != n_tiles - 1)
    halo_p = jnp.where(has_prev, prev_ref[...], jnp.zeros(prev_ref.shape, prev_ref.dtype))
    halo_n = jnp.where(has_next, next_ref[...], jnp.zeros(next_ref.shape, next_ref.dtype))
    ext = jnp.concatenate([xb, halo_p, halo_n, jnp.zeros((LANES - 32, xb.shape[1]), xb.dtype)], axis=0)
    shifted = jnp.dot(shift_ref[...], ext, preferred_element_type=F32)
    xm, xp = shifted[:TM], shifted[TM:]
    y = _silu(cw_ref[0:1, :] * xm + cw_ref[1:2, :] * xb.astype(F32) + cw_ref[2:3, :] * xp)

    cos, sina, sinb = cos_ref[...], sina_ref[...], sinb_ref[...]
    for h in range(DN_HEADS):
        for base, o_ref, scale in ((0, q_ref, DN_DIM ** -0.5), (DN_W, k_ref, None)):
            cs = slice(base + h * DN_DIM, base + (h + 1) * DN_DIM)
            z = y[:, cs]
            z = z * lax.rsqrt(jnp.sum(z * z, axis=-1, keepdims=True) + EPS)
            z = z * cos + pltpu.roll(z, 3 * DN_DIM // 4, 1) * sina + pltpu.roll(z, DN_DIM // 4, 1) * sinb
            if scale is not None:
                z = z * scale
            o_ref[:, h * DN_DIM:(h + 1) * DN_DIM] = z.astype(o_ref.dtype)
    v_ref[...] = y[:, 2 * DN_W:].astype(v_ref.dtype)

    sm = sm_ref[...]
    lane = lax.broadcasted_iota(jnp.int32, sm.shape, 1)
    logg = -jnp.exp(alog_ref[...]) * jax.nn.softplus(sm + dtb_ref[...])
    gam = jnp.where(lane < 3 * DN_HEADS, *_chunk_scans(logg))
    col = jnp.where(lane < 2 * DN_HEADS, jax.nn.sigmoid(sm), gam)
    col_ref[...] = col
    row_ref[...] = col.T[0:4 * DN_HEADS, :]


def _rope_tables(lc, seq):
    t = jnp.arange(seq)
    rowp = (t // GRID_W).astype(F32)
    colp = (t % GRID_W).astype(F32)
    n_freq = DN_DIM // 4
    inv = ROPE_BASE ** (-jnp.arange(n_freq, dtype=F32) / n_freq)
    ar, ac = rowp[:, None] * inv, colp[:, None] * inv
    ang = jnp.concatenate([ar, ar, ac, ac], axis=-1)
    cos, sin = jnp.cos(ang), jnp.sin(ang)
    quarter = (jnp.arange(DN_DIM) // n_freq) % 2
    sina = jnp.where(quarter == 0, -sin, 0.0)
    sinb = jnp.where(quarter == 1, sin, 0.0)
    pad = lambda a, v: jnp.concatenate([jnp.full((lc, DN_DIM), v, F32), a], axis=0)
    return pad(cos, 1.0), pad(sina, 0.0), pad(sinb, 0.0)


def _dn_prep(qkv_b, small, conv_w, a_log, dt_bias, rope):
    b, t, _ = qkv_b.shape
    n_tiles = t // TM
    hb = TM // 16
    pad16 = lambda a: jnp.zeros((1, LANES), F32).at[0, 2 * DN_HEADS:4 * DN_HEADS].set(a.reshape(-1).astype(F32))
    outs = [(DN_W, BF16), (DN_W, BF16), (DN_W, BF16), (LANES, F32)]
    res = pl.pallas_call(
        functools.partial(_dn_prep_kernel, n_tiles=n_tiles),
        grid=(b, n_tiles),
        in_specs=[_tok_spec(3 * DN_W),
                  pl.BlockSpec((None, 16, 3 * DN_W), lambda bi, j: (bi, jnp.maximum(j * hb - 1, 0), 0)),
                  pl.BlockSpec((None, 16, 3 * DN_W), lambda bi, j: (bi, jnp.minimum((j + 1) * hb, t // 16 - 1), 0)),
                  _tok_spec(LANES),
                  _full_spec((2 * TM, TM + LANES)),
                  _full_spec((3, 3 * DN_W)),
                  pl.BlockSpec((TM, DN_DIM), lambda bi, j: (j, 0)),
                  pl.BlockSpec((TM, DN_DIM), lambda bi, j: (j, 0)),
                  pl.BlockSpec((TM, DN_DIM), lambda bi, j: (j, 0)),
                  _full_spec((1, LANES)), _full_spec((1, LANES))],
        out_specs=[_tok_spec(c) for c, _ in outs] + [pl.BlockSpec((None, 4 * DN_HEADS, TM), lambda bi, j: (bi, 0, j))],
        out_shape=[jax.ShapeDtypeStruct((b, t, c), dt) for c, dt in outs]
        + [jax.ShapeDtypeStruct((b, 4 * DN_HEADS, t), F32)],
        compiler_params=_cparams(2), name="dn_prep",
    )(qkv_b, qkv_b, qkv_b, small, _conv_shift_matrix(), conv_w.astype(F32), *rope, pad16(a_log), pad16(dt_bias))
    q, k, v, colsm, rowsm = res
    n_chunks = t // DN_CHUNK
    rowsm = rowsm.reshape(b, 4 * DN_HEADS, n_chunks, DN_CHUNK).transpose(2, 0, 1, 3)
    rowsm = rowsm.reshape(n_chunks, b, 2 * DN_HEADS, 2 * DN_CHUNK)
    return q, k, v, colsm, rowsm


def _block_diag(x):
    n, w = x.shape
    x2 = jnp.concatenate([x, x], axis=0)
    row = lax.broadcasted_iota(jnp.int32, x2.shape, 0)
    lane = lax.broadcasted_iota(jnp.int32, x2.shape, 1)
    return jnp.where((row // n) == (lane // (w // 2)), x2, jnp.zeros_like(x2))


def _dn_pair_step(d, p, q_all, k_all, v_all, cm, rm, s_prev):
    c = DN_CHUNK
    heads = (2 * p, 2 * p + 1)
    cs2 = slice(2 * p * DN_DIM, (2 * p + 2) * DN_DIM)
    q2, k2, v2 = q_all[:, cs2], k_all[:, cs2], v_all[:, cs2]
    lane = lax.broadcasted_iota(jnp.int32, (c, 2 * c), 1)
    ri = lax.broadcasted_iota(jnp.int32, (c, 2 * c), 0)
    ci = lane % c
    first = lane < c
    col = lambda base: [cm[:, base + h:base + h + 1] for h in heads]
    beta, gam_c = col(d * DN_HEADS), col(2 * DN_HEADS + d * DN_HEADS)
    side = lambda ab: jnp.where(first, ab[0], ab[1])
    ir = (2 * DN_HEADS + d * DN_HEADS) // 2 + p
    gam_r = rm[ir:ir + 1, :]
    if d == 0:
        incl, strict, g_last = ri >= ci, ri > ci, [g[c - 1:c, :] for g in gam_c]
    else:
        incl, strict, g_last = ri <= ci, ri < ci, [g[0:1, :] for g in gam_c]
    decay = jnp.exp(jnp.where(incl, side(gam_c) - gam_r, -jnp.inf))
    zk = jnp.zeros((c, DN_DIM), BF16)
    kd = jnp.concatenate([jnp.concatenate([k2[:, :DN_DIM], zk], axis=1),
                          jnp.concatenate([zk, k2[:, DN_DIM:]], axis=1)], axis=0)
    g = lax.dot_general(jnp.concatenate([k2, q2], axis=0), kd, (((1,), (1,)), ((), ())),
                        preferred_element_type=F32)
    yield
    kk, qk = g[:c], g[c:]
    lneg = jnp.where(strict, -(side(beta) * kk * decay), 0.0)
    aqk = qk * decay
    eg = [jnp.exp(gc) for gc in gam_c]
    kf = [k2[:, i * DN_DIM:(i + 1) * DN_DIM].astype(F32) for i in range(2)]
    vf = [v2[:, i * DN_DIM:(i + 1) * DN_DIM].astype(F32) for i in range(2)]
    qf = [q2[:, i * DN_DIM:(i + 1) * DN_DIM].astype(F32) for i in range(2)]
    rhs = [jnp.concatenate([(beta[i] * eg[i]) * kf[i], beta[i] * vf[i]], axis=1).astype(BF16) for i in range(2)]
    eye = (ri == ci).astype(F32)
    same_blk = lambda s: (ri // s) == (ci // s)
    dneg = jnp.where(same_blk(4), lneg, 0.0)
    dnb = dneg.astype(BF16)
    dsq = jnp.dot(dnb, _block_diag(dnb), preferred_element_type=F32)
    yield
    t = (eye + dneg) + jnp.dot((eye + dneg).astype(BF16), _block_diag(dsq.astype(BF16)), preferred_element_type=F32)
    yield
    s = 4
    while s < c:
        eneg = jnp.where(jnp.logical_and(same_blk(2 * s), jnp.logical_not(same_blk(s))), lneg, 0.0)
        tb = t.astype(BF16)
        te = jnp.dot(tb, _block_diag(eneg.astype(BF16)), preferred_element_type=F32)
        yield
        t = t + jnp.dot(te.astype(BF16), _block_diag(tb), preferred_element_type=F32)
        yield
        s *= 2
    sol = jnp.dot(_block_diag(t.astype(BF16)), jnp.concatenate(rhs, axis=0), preferred_element_type=F32).astype(BF16)
    yield
    kdec = jnp.concatenate([kf[i] * jnp.exp(g_last[i] - gam_c[i]) for i in range(2)], axis=0)
    lhs = jnp.concatenate([_block_diag(aqk.astype(BF16)), _block_diag(kdec.T.astype(BF16))], axis=0)
    res = jnp.dot(lhs, sol, preferred_element_type=F32)
    yield
    outs, states = [], []
    for i in range(2):
        qo = res[i * c:(i + 1) * c]
        ab = res[2 * c + i * DN_DIM:2 * c + (i + 1) * DN_DIM]
        qeff = qf[i] * eg[i] - qo[:, :DN_DIM]
        both = jnp.dot(jnp.concatenate([qeff, ab[:, :DN_DIM]], axis=0).astype(BF16), s_prev[i].astype(BF16),
                       preferred_element_type=F32)
        outs.append(both[:c] + qo[:, DN_DIM:])
        states.append(jnp.exp(g_last[i]) * s_prev[i] - both[c:] + ab[:, DN_DIM:])
    return outs, states


def _dn_chain_kernel(qf_ref, kf_ref, vf_ref, cf_ref, rf_ref, qb_ref, kb_ref, vb_ref, cb_ref, rb_ref,
                     of_ref, ob_ref, s_ref, *, nb):
    @pl.when(pl.program_id(0) == 0)
    def _():
        s_ref[...] = jnp.zeros_like(s_ref)

    in_refs = ((qf_ref, kf_ref, vf_ref, cf_ref, rf_ref), (qb_ref, kb_ref, vb_ref, cb_ref, rb_ref))
    per_iter = DN_BATCH_UNROLL if nb % DN_BATCH_UNROLL == 0 else 1

    def body(i, carry):
        bs = [i * per_iter + u for u in range(per_iter)]
        n_pairs = DN_HEADS // 2
        gens = [_dn_pair_step(d, p, *[r[b] for r in in_refs[d]], [s_ref[d, b, 2 * p], s_ref[d, b, 2 * p + 1]])
                for b in bs for d in range(2) for p in range(n_pairs)]
        res = _lockstep(gens)
        for ib, b in enumerate(bs):
            for d, o_r in enumerate((of_ref, ob_ref)):
                base = (ib * 2 + d) * n_pairs
                o_r[b] = jnp.concatenate([o for p in range(n_pairs) for o in res[base + p][0]], axis=1)
                for p in range(n_pairs):
                    for i in range(2):
                        s_ref[d, b, 2 * p + i] = res[base + p][1][i]
        return carry

    lax.fori_loop(0, nb // per_iter, body, 0)


def _dn_chain(q, k, v, colsm, rowsm, lc):
    b, t, _ = q.shape
    n_chunks = t // DN_CHUNK
    nc_ctx = lc // DN_CHUNK
    fwd = lambda i: i
    bwd = lambda i: jnp.where(i < nc_ctx, nc_ctx - 1 - i, n_chunks - 1 + nc_ctx - i)
    tok = lambda cols, cm: pl.BlockSpec((b, DN_CHUNK, cols), lambda i: (0, cm(i), 0))
    rowspec = lambda cm: pl.BlockSpec((None, b, 2 * DN_HEADS, 2 * DN_CHUNK), lambda i: (cm(i), 0, 0, 0))
    in_specs = []
    for cm in (fwd, bwd):
        in_specs += [tok(DN_W, cm), tok(DN_W, cm), tok(DN_W, cm), tok(LANES, cm), rowspec(cm)]
    return pl.pallas_call(
        functools.partial(_dn_chain_kernel, nb=b),
        grid=(n_chunks,),
        in_specs=in_specs,
        out_specs=[tok(DN_W, fwd), tok(DN_W, bwd)],
        out_shape=[jax.ShapeDtypeStruct((b, t, DN_W), F32)] * 2,
        scratch_shapes=[pltpu.VMEM((2, b, DN_HEADS, DN_DIM, DN_DIM), F32)],
        compiler_params=pltpu.CompilerParams(dimension_semantics=("arbitrary",), vmem_limit_bytes=VMEM_LIMIT),
        name="dn_chain",
    )(q, k, v, colsm, rowsm, q, k, v, colsm, rowsm)


def _ffn_math(x, g, sc, sh, w1_ref, w2_ref, gt, gf, final_norm):
    h = _norm_mod(x, g, sc, sh).astype(BF16)
    a = jnp.maximum(jnp.dot(h, w1_ref[...], preferred_element_type=F32), 0.0)
    y = jnp.dot((a * a).astype(BF16), w2_ref[...], preferred_element_type=F32)
    out = x + gt * y
    if final_norm:
        out = out * lax.rsqrt(jnp.mean(out * out, axis=-1, keepdims=True) + EPS) * gf
    return out


def _even_out_ffn_kernel(*refs, n_x, final_norm):
    (att_ref, of_ref, ob_ref, gate_ref, gout_ref, w_ref, gt1_ref,
     g_ref, sc_ref, sh_ref, w1_ref, w2_ref, gt2_ref, gf_ref, o_ref) = refs[n_x:]
    o = of_ref[...] + ob_ref[...]
    gate = gate_ref[...].astype(F32)
    parts = [att_ref[...]]
    for h in range(DN_HEADS):
        cs = slice(h * DN_DIM, (h + 1) * DN_DIM)
        oh = o[:, cs]
        yh = oh * lax.rsqrt(jnp.mean(oh * oh, axis=-1, keepdims=True) + EPS) * gout_ref[...]
        parts.append((yh * _silu(gate[:, cs])).astype(BF16))
    a = jnp.concatenate(parts, axis=1)
    y = jnp.dot(a, w_ref[...], preferred_element_type=F32)
    x1 = _stream_tile(refs[:n_x]) + gt1_ref[...] * y
    o_ref[...] = _ffn_math(x1, g_ref[...], sc_ref[...], sh_ref[...], w1_ref, w2_ref, gt2_ref[...], gf_ref[...],
                           final_norm)


def _even_out_ffn(xs, att, o_f, o_b, gate, g_out, w_out, e, g2, mods, w1, w2, l, g_final, ctx_out, final_norm):
    b, t, d = _stream_shape(xs)
    off = 0 if ctx_out else 1
    assert ctx_out or not isinstance(xs, tuple)
    n_t = t // TM - off
    mod = lambda kind: _mod_spec(d, ctx_out, l, kind)
    x_args = _stream_args(xs)
    return pl.pallas_call(
        functools.partial(_even_out_ffn_kernel, n_x=len(x_args), final_norm=final_norm),
        grid=(b, n_t),
        in_specs=_stream_specs(xs, d, off) + [
            _tok_spec(A_W), _tok_spec(DN_W, off), _tok_spec(DN_W, off), _tok_spec(DN_W, off),
            _full_spec((1, DN_DIM)), _layer_spec(w_out, e), mod(GT1),
            _full_spec((1, d)), mod(SC2), mod(SH2), _layer_spec(w1, l), _layer_spec(w2, l), mod(GT2),
            _full_spec((1, d))],
        out_specs=_tok_spec(d),
        out_shape=jax.ShapeDtypeStruct((b, n_t * TM, d), F32),
        compiler_params=_cparams(2), name="even_out_ffn",
    )(*x_args, att, o_f, o_b, gate, g_out, w_out, mods, g2, mods, mods, w1, w2, mods, g_final)


def _gelu(x):
    return 0.5 * x * (1.0 + lax.erf(x * (2.0 ** -0.5)))


def _skewed(gens):
    results = [None] * len(gens)
    live, started = [], 0
    while started < len(gens) or live:
        if started < len(gens):
            live.append(started)
            started += 1
        still = []
        for i in reversed(live):
            try:
                next(gens[i])
                still.append(i)
            except StopIteration as stop:
                results[i] = stop.value
        live = still[::-1]
    return results


def _gmlp_ffn_kernel(x_ref, g_ref, sc_ref, sh_ref, wi_ref, gv_ref, ws_ref, bs_ref, wo_ref, gt_ref,
                     g2_ref, sc2_ref, sh2_ref, w1_ref, w2_ref, gt2_ref, gf_ref, o_ref, u_ref, v_ref, *, final_norm):
    x = x_ref[...]
    h = _norm_mod(x, g_ref[...], sc_ref[...], sh_ref[...]).astype(BF16)
    half = wi_ref.shape[1] // 2
    gw = half // GM_GROUPS

    pw = 2 * gw
    n_pairs = GM_GROUPS // 2

    def project(c0, dst_ref, want_ssq):
        z = jnp.dot(h, wi_ref[:, c0:c0 + pw], preferred_element_type=F32)
        yield
        z = _gelu(z)
        d0 = c0 % half
        dst_ref[:, d0:d0 + pw] = z.astype(dst_ref.dtype)
        return jnp.sum(z * z, axis=-1, keepdims=True) if want_ssq else None

    cols = [(half + p * pw, v_ref, True) for p in range(n_pairs)] + [(p * pw, u_ref, False) for p in range(n_pairs)]
    ssq = functools.reduce(jnp.add, _skewed([project(*c) for c in cols])[:n_pairs])
    rinv = lax.rsqrt(ssq * (1.0 / half) + EPS)

    def mix(p):
        cs = slice(p * pw, (p + 1) * pw)
        vb = (v_ref[:, cs] * rinv * gv_ref[:, cs]).astype(BF16)
        mixed = [[jnp.dot(ws_ref[2 * p + k], vb[n * GM_CHUNK:(n + 1) * GM_CHUNK, k * gw:(k + 1) * gw],
                          preferred_element_type=F32) for k in range(2)] for n in range(TM // GM_CHUNK)]
        yield
        bias = [jnp.concatenate([bs_ref[2 * p + k]] * (gw // LANES), axis=1) for k in range(2)]
        t = jnp.concatenate([jnp.concatenate([row[k] + bias[k] for k in range(2)], axis=1) for row in mixed], axis=0)
        t = (t * u_ref[:, cs]).astype(BF16)
        return jnp.dot(t, wo_ref[cs, :], preferred_element_type=F32)

    y = functools.reduce(jnp.add, _skewed([mix(p) for p in range(n_pairs)]))
    x1 = x + gt_ref[...] * y
    o_ref[...] = _ffn_math(x1, g2_ref[...], sc2_ref[...], sh2_ref[...], w1_ref, w2_ref, gt2_ref[...], gf_ref[...],
                           final_norm)


def _gmlp_ffn(x, g, mods, l, w_in, g_v, ws, bs, w_out, o, g2, w1, w2, g_final, has_ctx, final_norm):
    b, t, d = x.shape
    half = w_in.shape[2] // 2
    assert (half // GM_GROUPS) % LANES == 0
    mod = lambda kind: _mod_spec(d, has_ctx, l, kind)
    return pl.pallas_call(
        functools.partial(_gmlp_ffn_kernel, final_norm=final_norm),
        grid=(b, t // TM),
        in_specs=[_tok_spec(d), _full_spec((1, d)), mod(SC1), mod(SH1),
                  _layer_spec(w_in, o), _full_spec((1, half)), _layer_spec(ws, o), _full_spec(bs.shape),
                  _layer_spec(w_out, o), mod(GT1),
                  _full_spec((1, d)), mod(SC2), mod(SH2), _layer_spec(w1, l), _layer_spec(w2, l), mod(GT2),
                  _full_spec((1, d))],
        out_specs=_tok_spec(d),
        out_shape=jax.ShapeDtypeStruct((b, t, d), F32),
        scratch_shapes=[pltpu.VMEM((TM, half), F32), pltpu.VMEM((TM, half), F32)],
        compiler_params=_cparams(2), name="gmlp_ffn",
    )(x, g, mods, mods, w_in, g_v, ws, bs, w_out, mods, g2, mods, mods, w1, w2, mods, g_final)


def kernel(x, c, ctx, c_ctx, w_ada, b_ada, g_norm_mix, g_norm_ffn, w_in_even, w_out_even, na_rpb, dn_conv, dn_a_log,
           dn_dt_bias, dn_g_out, w_in_odd, gm_g_v, gm_ws, gm_bs, w_out_odd, w_ff1, w_ff2, g_final):
    nb, seq, d = x.shape
    lc = ctx.shape[1]
    depth = w_ada.shape[0]
    assert nb <= CTX_ROW and seq % TM == 0 and lc == TM and seq % GRID_W == 0

    cond = jnp.zeros((MOD_ROWS, d), F32).at[:nb].set(c).at[CTX_ROW].set(c_ctx)
    mods = _adaln(cond, w_ada, b_ada)
    rope = _rope_tables(lc, seq)
    row = lambda a: a.reshape(1, -1).astype(F32)

    w_in_e = jnp.pad(w_in_even, ((0, 0), (0, 0), (0, LANES - 4 * DN_HEADS))).astype(BF16)
    w_out_e, w_in_o, w_out_o, ws_o = (w.astype(BF16) for w in (w_out_even, w_in_odd, w_out_odd, gm_ws))
    w1, w2 = w_ff1.astype(BF16), w_ff2.astype(BF16)

    xs = (ctx, x)
    has_ctx = True
    for l in range(depth):
        ctx_live = any(j % 2 == 0 for j in range(l + 1, depth))
        if l % 2 == 0:
            assert has_ctx
            e = l // 2
            qkv_a, qkv_b, gate, small = _even_in(xs, row(g_norm_mix[l]), mods, l, w_in_e, e)
            att = _na_attention(qkv_a, _na_bias_table(na_rpb[e]), lc, ctx_live)
            q, k, v, colsm, rowsm = _dn_prep(qkv_b, small, dn_conv[e], dn_a_log[e], dn_dt_bias[e], rope)
            o_f, o_b = _dn_chain(q, k, v, colsm, rowsm, lc)
            if isinstance(xs, tuple) and not ctx_live:
                xs = jnp.concatenate(xs, axis=1)
            xs = _even_out_ffn(xs, att, o_f, o_b, gate, row(dn_g_out[e]), w_out_e, e, row(g_norm_ffn[l]), mods,
                               w1, w2, l, row(g_final), ctx_live, l == depth - 1)
            has_ctx = ctx_live
        else:
            if isinstance(xs, tuple):
                xs = jnp.concatenate(xs, axis=1)
            o = l // 2
            bs = jnp.broadcast_to(gm_bs[o].astype(F32)[:, :, None], (GM_GROUPS, GM_CHUNK, LANES))
            xs = _gmlp_ffn(xs, row(g_norm_mix[l]), mods, l, w_in_o, row(gm_g_v[o]), ws_o, bs, w_out_o, o,
                           row(g_norm_ffn[l]), w1, w2, row(g_final), has_ctx, l == depth - 1)
    return xs[:, lc:] if has_ctx else xs
```

```python
import functools

import jax
import jax.numpy as jnp
from jax import lax
from jax.experimental import pallas as pl
from jax.experimental.pallas import tpu as pltpu

EPS = 1e-6
GRID_W = 64
NA_HEADS, NA_DIM, NA_WIN_R, NA_WIN_C = 8, 64, 8, 16
A_W = NA_HEADS * NA_DIM
DN_HEADS, DN_DIM, DN_CHUNK = 4, 128, 64
DN_W = DN_HEADS * DN_DIM
ROPE_BASE = 10000.0
GM_GROUPS, GM_CHUNK = 8, 128

LANES = 128
TM = 256
DN_BATCH_UNROLL = 8
NA_ROWS_PER_STEP = 4
NA_CHAINS_IN_FLIGHT = 4
VMEM_LIMIT = 56 * 1024 * 1024
MOD_ROWS = 16
CTX_ROW = 8
SH1, SC1, GT1, SH2, SC2, GT2 = range(6)
F32, BF16 = jnp.float32, jnp.bfloat16


def _cparams(n_axes):
    return pltpu.CompilerParams(dimension_semantics=("parallel",) * n_axes, vmem_limit_bytes=VMEM_LIMIT)


def _tok_spec(cols, off=0):
    return pl.BlockSpec((None, TM, cols), lambda b, j: (b, j + off, 0))


def _mod_spec(d, ctx_tile, l, kind):
    if ctx_tile:
        return pl.BlockSpec((None, None, None, 1, d), lambda b, j: (l, kind, jnp.where(j == 0, CTX_ROW, b), 0, 0))
    return pl.BlockSpec((None, None, None, 1, d), lambda b, j: (l, kind, b, 0, 0))


def _full_spec(shape):
    n = len(shape)
    return pl.BlockSpec(shape, lambda b, j: (0,) * n)


def _layer_spec(w, l):
    return pl.BlockSpec((None,) + w.shape[1:], lambda b, j: (l,) + (0,) * (w.ndim - 1), pipeline_mode=pl.Buffered(1))


def _stream_specs(xs, d, off=0):
    if isinstance(xs, tuple):
        return [pl.BlockSpec((None, TM, d), lambda b, j: (b, 0, 0)),
                pl.BlockSpec((None, TM, d), lambda b, j: (b, jnp.maximum(j - 1, 0), 0))]
    return [_tok_spec(d, off)]


def _stream_tile(refs):
    if len(refs) == 2:
        return jnp.where(pl.program_id(1) == 0, refs[0][...], refs[1][...])
    return refs[0][...]


def _stream_args(xs):
    return list(xs) if isinstance(xs, tuple) else [xs]


def _stream_shape(xs):
    if isinstance(xs, tuple):
        return xs[1].shape[0], xs[0].shape[1] + xs[1].shape[1], xs[1].shape[2]
    return xs.shape


def _norm_mod(x, g, sc, sh):
    y = x * lax.rsqrt(jnp.mean(x * x, axis=-1, keepdims=True) + EPS)
    return (y * g) * (1.0 + sc) + sh


def _silu(x):
    return x * jax.nn.sigmoid(x)


def _adaln_kernel(cond_ref, w_ref, b_ref, o_ref):
    a = _silu(cond_ref[...]).astype(BF16)
    m = jnp.dot(a, w_ref[...].astype(BF16), preferred_element_type=F32) + b_ref[...]
    for r in range(MOD_ROWS):
        o_ref[r] = m[r:r + 1, :]


def _adaln(cond, w_ada, b_ada):
    depth, d, n = w_ada.shape
    assert n == 6 * d
    return pl.pallas_call(
        _adaln_kernel,
        grid=(depth, 6),
        in_specs=[pl.BlockSpec((MOD_ROWS, d), lambda l, j: (0, 0)),
                  pl.BlockSpec((None, d, d), lambda l, j: (l, 0, j)),
                  pl.BlockSpec((None, 1, d), lambda l, j: (l, 0, j))],
        out_specs=pl.BlockSpec((None, None, MOD_ROWS, 1, d), lambda l, j: (l, j, 0, 0, 0)),
        out_shape=jax.ShapeDtypeStruct((depth, 6, MOD_ROWS, 1, d), F32),
        compiler_params=_cparams(2), name="adaln",
    )(cond, w_ada, b_ada.reshape(depth, 1, n))


def _even_in_kernel(*refs, n_x):
    g_ref, sc_ref, sh_ref, w_ref, qa_ref, qb_ref, gate_ref, sm_ref = refs[n_x:]
    h = _norm_mod(_stream_tile(refs[:n_x]), g_ref[...], sc_ref[...], sh_ref[...]).astype(BF16)
    p = jnp.dot(h, w_ref[...], preferred_element_type=F32)
    qa_ref[...] = p[:, :3 * A_W].astype(BF16)
    qb_ref[...] = p[:, 3 * A_W:3 * A_W + 3 * DN_W].astype(BF16)
    gate_ref[...] = p[:, 3 * A_W + 3 * DN_W:3 * A_W + 4 * DN_W].astype(BF16)
    sm_ref[...] = p[:, 3 * A_W + 4 * DN_W:]


def _even_in(xs, g, mods, l, w_pad, e):
    b, t, d = _stream_shape(xs)
    outs = [(3 * A_W, BF16), (3 * DN_W, BF16), (DN_W, BF16), (LANES, F32)]
    x_args = _stream_args(xs)
    return pl.pallas_call(
        functools.partial(_even_in_kernel, n_x=len(x_args)),
        grid=(b, t // TM),
        in_specs=_stream_specs(xs, d) + [_full_spec((1, d)), _mod_spec(d, True, l, SC1), _mod_spec(d, True, l, SH1),
                                         _layer_spec(w_pad, e)],
        out_specs=[_tok_spec(c) for c, _ in outs],
        out_shape=[jax.ShapeDtypeStruct((b, t, c), dt) for c, dt in outs],
        compiler_params=_cparams(2), name="even_in",
    )(*x_args, g, mods, mods, w_pad)


def _lockstep(gens):
    results = [None] * len(gens)
    live = list(range(len(gens)))
    while live:
        still = []
        for i in live:
            try:
                next(gens[i])
                still.append(i)
            except StopIteration as stop:
                results[i] = stop.value
        live = still
    return results


def _softmax_head_pair(q, keys, vals, biases):
    wq = q.shape[0]
    lane = lax.broadcasted_iota(jnp.int32, (2 * wq, LANES), 1)
    row = lax.broadcasted_iota(jnp.int32, (2 * wq, LANES), 0)
    own = (lane // NA_DIM) == (row // wq)
    q2 = jnp.concatenate([q, q], axis=0)
    qm = jnp.where(own, q2, jnp.zeros_like(q2))
    ss = [lax.dot_general(qm, kb, (((1,), (1,)), ((), ())), preferred_element_type=F32) for kb in keys]
    yield
    ss = [s if bias is None else s + bias for s, bias in zip(ss, biases)]
    m = functools.reduce(jnp.maximum, [jnp.max(s, axis=-1, keepdims=True) for s in ss])
    yield
    ps = [jnp.exp(s - m) for s in ss]
    den = functools.reduce(jnp.add, [jnp.sum(p, axis=-1, keepdims=True) for p in ps])
    yield
    o = functools.reduce(jnp.add, [jnp.dot(p.astype(BF16), vb, preferred_element_type=F32)
                                   for p, vb in zip(ps, vals)])
    yield
    o = o / den
    return jnp.where(lax.broadcasted_iota(jnp.int32, (wq, LANES), 1) < NA_DIM, o[:wq], o[wq:])


def _attend_jobs(q_ref, o_ref, jobs):
    scale = NA_DIM ** -0.5
    gens, dests = [], []
    for q_rows, o_rows, blocks, bias_of in jobs:
        for pair in range(NA_HEADS // 2):
            cs = slice(pair * LANES, (pair + 1) * LANES)
            keys, vals = blocks(cs)
            gens.append(_softmax_head_pair(q_ref[q_rows, cs] * scale, keys, vals, bias_of(pair)))
            dests.append((o_rows, cs))
    for g0 in range(0, len(gens), NA_CHAINS_IN_FLIGHT):
        outs = _lockstep(gens[g0:g0 + NA_CHAINS_IN_FLIGHT])
        for (o_rows, cs), o in zip(dests[g0:g0 + NA_CHAINS_IN_FLIGHT], outs):
            o_ref[o_rows, cs] = o.astype(o_ref.dtype)


def _na_kernel(q_ref, k_ref, v_ref, bias_ref, o_ref, *, lc, rows, ctx_out):
    n_loc = NA_WIN_R * GRID_W
    out_off = lc if ctx_out else 0

    if ctx_out:
        blocks = lambda cs: ([k_ref[0:lc, cs]], [v_ref[0:lc, cs]])
        _attend_jobs(q_ref, o_ref, [(slice(0, lc), slice(0, lc), blocks, lambda pair: [None])])

    def row_group(g, carry):
        jobs = []
        for i in range(NA_ROWS_PER_STEP):
            r = g * NA_ROWS_PER_STEP + i
            rs = jnp.clip(r - NA_WIN_R // 2, 0, rows - NA_WIN_R)
            start = pl.multiple_of(lc + rs * GRID_W, GRID_W)
            blocks = lambda cs, start=start: ([k_ref[pl.ds(start, n_loc), cs], k_ref[0:lc, cs]],
                                              [v_ref[pl.ds(start, n_loc), cs], v_ref[0:lc, cs]])
            bias_of = lambda pair, var=r - rs: [bias_ref[var, 2 * pair:2 * pair + 2].reshape(2 * GRID_W, n_loc), None]
            q_rows = pl.ds(pl.multiple_of(lc + r * GRID_W, GRID_W), GRID_W)
            o_rows = pl.ds(pl.multiple_of(out_off + r * GRID_W, GRID_W), GRID_W)
            jobs.append((q_rows, o_rows, blocks, bias_of))
        _attend_jobs(q_ref, o_ref, jobs)
        return carry

    lax.fori_loop(0, rows // NA_ROWS_PER_STEP, row_group, 0)


def _na_bias_table(rpb):
    col = jnp.arange(GRID_W)
    cstart = jnp.clip(col - NA_WIN_C // 2, 0, GRID_W - NA_WIN_C)
    col_ok = (col[None, :] >= cstart[:, None]) & (col[None, :] < cstart[:, None] + NA_WIN_C)
    dc_idx = jnp.clip(col[None, :] - col[:, None], -(NA_WIN_C - 1), NA_WIN_C - 1) + NA_WIN_C - 1
    onehot = (dc_idx[:, :, None] == jnp.arange(2 * NA_WIN_C - 1)).astype(F32)
    v_i, w_i = jnp.arange(NA_WIN_R)[:, None], jnp.arange(NA_WIN_R)[None, :]
    rowhot = ((NA_WIN_R - 1 - v_i + w_i)[:, :, None] == jnp.arange(2 * NA_WIN_R - 1)).astype(F32)
    tab = jnp.einsum('hij,vwi,qkj->vhqwk', rpb.astype(F32), rowhot, onehot, precision=lax.Precision.HIGHEST)
    tab = jnp.where(col_ok[None, None, :, None, :], tab, -jnp.inf)
    return tab.reshape(NA_WIN_R, NA_HEADS, GRID_W, NA_WIN_R * GRID_W)


def _na_attention(qkv_a, bias_tab, lc, ctx_out):
    b, t, _ = qkv_a.shape
    rows = (t - lc) // GRID_W
    assert rows >= NA_WIN_R and rows % NA_ROWS_PER_STEP == 0 and lc % GRID_W == 0
    n_out = t if ctx_out else t - lc
    col_block = lambda c: pl.BlockSpec((None, t, A_W), lambda bi: (bi, 0, c))
    return pl.pallas_call(
        functools.partial(_na_kernel, lc=lc, rows=rows, ctx_out=ctx_out),
        grid=(b,),
        in_specs=[col_block(0), col_block(1), col_block(2),
                  pl.BlockSpec(bias_tab.shape, lambda bi: (0, 0, 0, 0), pipeline_mode=pl.Buffered(1))],
        out_specs=pl.BlockSpec((None, n_out, A_W), lambda bi: (bi, 0, 0)),
        out_shape=jax.ShapeDtypeStruct((b, n_out, A_W), BF16),
        compiler_params=_cparams(1), name="na_attention",
    )(qkv_a, qkv_a, qkv_a, bias_tab)


def _chunk_scans(x):
    n = x.shape[0]
    pos = lax.broadcasted_iota(jnp.int32, x.shape, 0) % DN_CHUNK
    fwd, s = x, 1
    while s < DN_CHUNK:
        fwd = fwd + jnp.where(pos >= s, pltpu.roll(fwd, s, 0), 0.0)
        s *= 2
    grouped = fwd.reshape(n // DN_CHUNK, DN_CHUNK, x.shape[1])
    total = jnp.broadcast_to(grouped[:, DN_CHUNK - 1:DN_CHUNK, :], grouped.shape).reshape(x.shape)
    return fwd, total - fwd + x


def _conv_shift_matrix():
    r = lax.broadcasted_iota(jnp.int32, (2 * TM, TM + LANES), 0)
    c = lax.broadcasted_iota(jnp.int32, (2 * TM, TM + LANES), 1)
    down = (r < TM) & ((c == r - 1) | ((r == 0) & (c == TM + 15)))
    up = (r >= TM) & (((c == r - TM + 1) & (r < 2 * TM - 1)) | ((r == 2 * TM - 1) & (c == TM + 16)))
    return (down | up).astype(BF16)


def _dn_prep_kernel(x_ref, prev_ref, next_ref, sm_ref, shift_ref, cw_ref, cos_ref, sina_ref, sinb_ref, alog_ref,
                    dtb_ref, q_ref, k_ref, v_ref, col_ref, row_ref, *, n_tiles):
    j = pl.program_id(1)
    xb = x_ref[...]
    has_prev = jnp.logical_and(j != 0, j != 1)
    has_next = jnp.logical_and(j != 0, j != n_tiles - 1)
    halo_p = jnp.where(has_prev, prev_ref[...], jnp.zeros(prev_ref.shape, prev_ref.dtype))
    halo_n = jnp.where(has_next, next_ref[...], jnp.zeros(next_ref.shape, next_ref.dtype))
    ext = jnp.concatenate([xb, halo_p, halo_n, jnp.zeros((LANES - 32, xb.shape[1]), xb.dtype)], axis=0)
    shifted = jnp.dot(shift_ref[...], ext, preferred_element_type=F32)
    xm, xp = shifted[:TM], shifted[TM:]
    y = _silu(cw_ref[0:1, :] * xm + cw_ref[1:2, :] * xb.astype(F32) + cw_ref[2:3, :] * xp)

    cos, sina, sinb = cos_ref[...], sina_ref[...], sinb_ref[...]
    for h in range(DN_HEADS):
        for base, o_ref, scale in ((0, q_ref, DN_DIM ** -0.5), (DN_W, k_ref, None)):
            cs = slice(base + h * DN_DIM, base + (h + 1) * DN_DIM)
            z = y[:, cs]
            z = z * lax.rsqrt(jnp.sum(z * z, axis=-1, keepdims=True) + EPS)
            z = z * cos + pltpu.roll(z, 3 * DN_DIM // 4, 1) * sina + pltpu.roll(z, DN_DIM // 4, 1) * sinb
            if scale is not None:
                z = z * scale
            o_ref[:, h * DN_DIM:(h + 1) * DN_DIM] = z.astype(o_ref.dtype)
    v_ref[...] = y[:, 2 * DN_W:].astype(v_ref.dtype)

    sm = sm_ref[...]
    lane = lax.broadcasted_iota(jnp.int32, sm.shape, 1)
    logg = -jnp.exp(alog_ref[...]) * jax.nn.softplus(sm + dtb_ref[...])
    gam = jnp.where(lane < 3 * DN_HEADS, *_chunk_scans(logg))
    col = jnp.where(lane < 2 * DN_HEADS, jax.nn.sigmoid(sm), gam)
    col_ref[...] = col
    row_ref[...] = col.T[0:4 * DN_HEADS, :]


def _rope_tables(lc, seq):
    t = jnp.arange(seq)
    rowp = (t // GRID_W).astype(F32)
    colp = (t % GRID_W).astype(F32)
    n_freq = DN_DIM // 4
    inv = ROPE_BASE ** (-jnp.arange(n_freq, dtype=F32) / n_freq)
    ar, ac = rowp[:, None] * inv, colp[:, None] * inv
    ang = jnp.concatenate([ar, ar, ac, ac], axis=-1)
    cos, sin = jnp.cos(ang), jnp.sin(ang)
    quarter = (jnp.arange(DN_DIM) // n_freq) % 2
    sina = jnp.where(quarter == 0, -sin, 0.0)
    sinb = jnp.where(quarter == 1, sin, 0.0)
    pad = lambda a, v: jnp.concatenate([jnp.full((lc, DN_DIM), v, F32), a], axis=0)
    return pad(cos, 1.0), pad(sina, 0.0), pad(sinb, 0.0)


def _dn_prep(qkv_b, small, conv_w, a_log, dt_bias, rope):
    b, t, _ = qkv_b.shape
    n_tiles = t // TM
    hb = TM // 16
    pad16 = lambda a: jnp.zeros((1, LANES), F32).at[0, 2 * DN_HEADS:4 * DN_HEADS].set(a.reshape(-1).astype(F32))
    outs = [(DN_W, BF16), (DN_W, BF16), (DN_W, BF16), (LANES, F32)]
    res = pl.pallas_call(
        functools.partial(_dn_prep_kernel, n_tiles=n_tiles),
        grid=(b, n_tiles),
        in_specs=[_tok_spec(3 * DN_W),
                  pl.BlockSpec((None, 16, 3 * DN_W), lambda bi, j: (bi, jnp.maximum(j * hb - 1, 0), 0)),
                  pl.BlockSpec((None, 16, 3 * DN_W), lambda bi, j: (bi, jnp.minimum((j + 1) * hb, t // 16 - 1), 0)),
                  _tok_spec(LANES),
                  _full_spec((2 * TM, TM + LANES)),
                  _full_spec((3, 3 * DN_W)),
                  pl.BlockSpec((TM, DN_DIM), lambda bi, j: (j, 0)),
                  pl.BlockSpec((TM, DN_DIM), lambda bi, j: (j, 0)),
                  pl.BlockSpec((TM, DN_DIM), lambda bi, j: (j, 0)),
                  _full_spec((1, LANES)), _full_spec((1, LANES))],
        out_specs=[_tok_spec(c) for c, _ in outs] + [pl.BlockSpec((None, 4 * DN_HEADS, TM), lambda bi, j: (bi, 0, j))],
        out_shape=[jax.ShapeDtypeStruct((b, t, c), dt) for c, dt in outs]
        + [jax.ShapeDtypeStruct((b, 4 * DN_HEADS, t), F32)],
        compiler_params=_cparams(2), name="dn_prep",
    )(qkv_b, qkv_b, qkv_b, small, _conv_shift_matrix(), conv_w.astype(F32), *rope, pad16(a_log), pad16(dt_bias))
    q, k, v, colsm, rowsm = res
    n_chunks = t // DN_CHUNK
    rowsm = rowsm.reshape(b, 4 * DN_HEADS, n_chunks, DN_CHUNK).transpose(2, 0, 1, 3)
    rowsm = rowsm.reshape(n_chunks, b, 2 * DN_HEADS, 2 * DN_CHUNK)
    return q, k, v, colsm, rowsm


def _block_diag(x):
    n, w = x.shape
    x2 = jnp.concatenate([x, x], axis=0)
    row = lax.broadcasted_iota(jnp.int32, x2.shape, 0)
    lane = lax.broadcasted_iota(jnp.int32, x2.shape, 1)
    return jnp.where((row // n) == (lane // (w // 2)), x2, jnp.zeros_like(x2))


def _dn_pair_step(d, p, q_all, k_all, v_all, cm, rm, s_prev):
    c = DN_CHUNK
    heads = (2 * p, 2 * p + 1)
    cs2 = slice(2 * p * DN_DIM, (2 * p + 2) * DN_DIM)
    q2, k2, v2 = q_all[:, cs2], k_all[:, cs2], v_all[:, cs2]
    lane = lax.broadcasted_iota(jnp.int32, (c, 2 * c), 1)
    ri = lax.broadcasted_iota(jnp.int32, (c, 2 * c), 0)
    ci = lane % c
    first = lane < c
    col = lambda base: [cm[:, base + h:base + h + 1] for h in heads]
    beta, gam_c = col(d * DN_HEADS), col(2 * DN_HEADS + d * DN_HEADS)
    side = lambda ab: jnp.where(first, ab[0], ab[1])
    ir = (2 * DN_HEADS + d * DN_HEADS) // 2 + p
    gam_r = rm[ir:ir + 1, :]
    if d == 0:
        incl, strict, g_last = ri >= ci, ri > ci, [g[c - 1:c, :] for g in gam_c]
    else:
        incl, strict, g_last = ri <= ci, ri < ci, [g[0:1, :] for g in gam_c]
    decay = jnp.exp(jnp.where(incl, side(gam_c) - gam_r, -jnp.inf))
    zk = jnp.zeros((c, DN_DIM), BF16)
    kd = jnp.concatenate([jnp.concatenate([k2[:, :DN_DIM], zk], axis=1),
                          jnp.concatenate([zk, k2[:, DN_DIM:]], axis=1)], axis=0)
    g = lax.dot_general(jnp.concatenate([k2, q2], axis=0), kd, (((1,), (1,)), ((), ())),
                        preferred_element_type=F32)
    yield
    kk, qk = g[:c], g[c:]
    lneg = jnp.where(strict, -(side(beta) * kk * decay), 0.0)
    aqk = qk * decay
    eg = [jnp.exp(gc) for gc in gam_c]
    kf = [k2[:, i * DN_DIM:(i + 1) * DN_DIM].astype(F32) for i in range(2)]
    vf = [v2[:, i * DN_DIM:(i + 1) * DN_DIM].astype(F32) for i in range(2)]
    qf = [q2[:, i * DN_DIM:(i + 1) * DN_DIM].astype(F32) for i in range(2)]
    rhs = [jnp.concatenate([(beta[i] * eg[i]) * kf[i], beta[i] * vf[i]], axis=1).astype(BF16) for i in range(2)]
    eye = (ri == ci).astype(F32)
    same_blk = lambda s: (ri // s) == (ci // s)
    dneg = jnp.where(same_blk(4), lneg, 0.0)
    dnb = dneg.astype(BF16)
    dsq = jnp.dot(dnb, _block_diag(dnb), preferred_element_type=F32)
    yield
    t = (eye + dneg) + jnp.dot((eye + dneg).astype(BF16), _block_diag(dsq.astype(BF16)), preferred_element_type=F32)
    yield
    s = 4
    while s < c:
        eneg = jnp.where(jnp.logical_and(same_blk(2 * s), jnp.logical_not(same_blk(s))), lneg, 0.0)
        tb = t.astype(BF16)
        te = jnp.dot(tb, _block_diag(eneg.astype(BF16)), preferred_element_type=F32)
        yield
        t = t + jnp.dot(te.astype(BF16), _block_diag(tb), preferred_element_type=F32)
        yield
        s *= 2
    sol = jnp.dot(_block_diag(t.astype(BF16)), jnp.concatenate(rhs, axis=0), preferred_element_type=F32).astype(BF16)
    yield
    kdec = jnp.concatenate([kf[i] * jnp.exp(g_last[i] - gam_c[i]) for i in range(2)], axis=0)
    lhs = jnp.concatenate([_block_diag(aqk.astype(BF16)), _block_diag(kdec.T.astype(BF16))], axis=0)
    res = jnp.dot(lhs, sol, preferred_element_type=F32)
    yield
    outs, states = [], []
    for i in range(2):
        qo = res[i * c:(i + 1) * c]
        ab = res[2 * c + i * DN_DIM:2 * c + (i + 1) * DN_DIM]
        qeff = qf[i] * eg[i] - qo[:, :DN_DIM]
        both = jnp.dot(jnp.concatenate([qeff, ab[:, :DN_DIM]], axis=0).astype(BF16), s_prev[i].astype(BF16),
                       preferred_element_type=F32)
        outs.append(both[:c] + qo[:, DN_DIM:])
        states.append(jnp.exp(g_last[i]) * s_prev[i] - both[c:] + ab[:, DN_DIM:])
    return outs, states


def _dn_chain_kernel(qf_ref, kf_ref, vf_ref, cf_ref, rf_ref, qb_ref, kb_ref, vb_ref, cb_ref, rb_ref,
                     of_ref, ob_ref, s_ref, *, nb):
    @pl.when(pl.program_id(0) == 0)
    def _():
        s_ref[...] = jnp.zeros_like(s_ref)

    in_refs = ((qf_ref, kf_ref, vf_ref, cf_ref, rf_ref), (qb_ref, kb_ref, vb_ref, cb_ref, rb_ref))
    per_iter = DN_BATCH_UNROLL if nb % DN_BATCH_UNROLL == 0 else 1

    def body(i, carry):
        bs = [i * per_iter + u for u in range(per_iter)]
        n_pairs = DN_HEADS // 2
        gens = [_dn_pair_step(d, p, *[r[b] for r in in_refs[d]], [s_ref[d, b, 2 * p], s_ref[d, b, 2 * p + 1]])
                for b in bs for d in range(2) for p in range(n_pairs)]
        res = _lockstep(gens)
        for ib, b in enumerate(bs):
            for d, o_r in enumerate((of_ref, ob_ref)):
                base = (ib * 2 + d) * n_pairs
                o_r[b] = jnp.concatenate([o for p in range(n_pairs) for o in res[base + p][0]], axis=1)
                for p in range(n_pairs):
                    for i in range(2):
                        s_ref[d, b, 2 * p + i] = res[base + p][1][i]
        return carry

    lax.fori_loop(0, nb // per_iter, body, 0)


def _dn_chain(q, k, v, colsm, rowsm, lc):
    b, t, _ = q.shape
    n_chunks = t // DN_CHUNK
    nc_ctx = lc // DN_CHUNK
    fwd = lambda i: i
    bwd = lambda i: jnp.where(i < nc_ctx, nc_ctx - 1 - i, n_chunks - 1 + nc_ctx - i)
    tok = lambda cols, cm: pl.BlockSpec((b, DN_CHUNK, cols), lambda i: (0, cm(i), 0))
    rowspec = lambda cm: pl.BlockSpec((None, b, 2 * DN_HEADS, 2 * DN_CHUNK), lambda i: (cm(i), 0, 0, 0))
    in_specs = []
    for cm in (fwd, bwd):
        in_specs += [tok(DN_W, cm), tok(DN_W, cm), tok(DN_W, cm), tok(LANES, cm), rowspec(cm)]
    return pl.pallas_call(
        functools.partial(_dn_chain_kernel, nb=b),
        grid=(n_chunks,),
        in_specs=in_specs,
        out_specs=[tok(DN_W, fwd), tok(DN_W, bwd)],
        out_shape=[jax.ShapeDtypeStruct((b, t, DN_W), F32)] * 2,
        scratch_shapes=[pltpu.VMEM((2, b, DN_HEADS, DN_DIM, DN_DIM), F32)],
        compiler_params=pltpu.CompilerParams(dimension_semantics=("arbitrary",), vmem_limit_bytes=VMEM_LIMIT),
        name="dn_chain",
    )(q, k, v, colsm, rowsm, q, k, v, colsm, rowsm)


def _ffn_math(x, g, sc, sh, w1_ref, w2_ref, gt, gf, final_norm):
    h = _norm_mod(x, g, sc, sh).astype(BF16)
    a = jnp.maximum(jnp.dot(h, w1_ref[...], preferred_element_type=F32), 0.0)
    y = jnp.dot((a * a).astype(BF16), w2_ref[...], preferred_element_type=F32)
    out = x + gt * y
    if final_norm:
        out = out * lax.rsqrt(jnp.mean(out * out, axis=-1, keepdims=True) + EPS) * gf
    return out


def _even_out_ffn_kernel(*refs, n_x, final_norm):
    (att_ref, of_ref, ob_ref, gate_ref, gout_ref, w_ref, gt1_ref,
     g_ref, sc_ref, sh_ref, w1_ref, w2_ref, gt2_ref, gf_ref, o_ref) = refs[n_x:]
    o = of_ref[...] + ob_ref[...]
    gate = gate_ref[...].astype(F32)
    parts = [att_ref[...]]
    for h in range(DN_HEADS):
        cs = slice(h * DN_DIM, (h + 1) * DN_DIM)
        oh = o[:, cs]
        yh = oh * lax.rsqrt(jnp.mean(oh * oh, axis=-1, keepdims=True) + EPS) * gout_ref[...]
        parts.append((yh * _silu(gate[:, cs])).astype(BF16))
    a = jnp.concatenate(parts, axis=1)
    y = jnp.dot(a, w_ref[...], preferred_element_type=F32)
    x1 = _stream_tile(refs[:n_x]) + gt1_ref[...] * y
    o_ref[...] = _ffn_math(x1, g_ref[...], sc_ref[...], sh_ref[...], w1_ref, w2_ref, gt2_ref[...], gf_ref[...],
                           final_norm)


def _even_out_ffn(xs, att, o_f, o_b, gate, g_out, w_out, e, g2, mods, w1, w2, l, g_final, ctx_out, final_norm):
    b, t, d = _stream_shape(xs)
    off = 0 if ctx_out else 1
    assert ctx_out or not isinstance(xs, tuple)
    n_t = t // TM - off
    mod = lambda kind: _mod_spec(d, ctx_out, l, kind)
    x_args = _stream_args(xs)
    return pl.pallas_call(
        functools.partial(_even_out_ffn_kernel, n_x=len(x_args), final_norm=final_norm),
        grid=(b, n_t),
        in_specs=_stream_specs(xs, d, off) + [
            _tok_spec(A_W), _tok_spec(DN_W, off), _tok_spec(DN_W, off), _tok_spec(DN_W, off),
            _full_spec((1, DN_DIM)), _layer_spec(w_out, e), mod(GT1),
            _full_spec((1, d)), mod(SC2), mod(SH2), _layer_spec(w1, l), _layer_spec(w2, l), mod(GT2),
            _full_spec((1, d))],
        out_specs=_tok_spec(d),
        out_shape=jax.ShapeDtypeStruct((b, n_t * TM, d), F32),
        compiler_params=_cparams(2), name="even_out_ffn",
    )(*x_args, att, o_f, o_b, gate, g_out, w_out, mods, g2, mods, mods, w1, w2, mods, g_final)


def _gelu(x):
    return 0.5 * x * (1.0 + lax.erf(x * (2.0 ** -0.5)))


def _skewed(gens):
    results = [None] * len(gens)
    live, started = [], 0
    while started < len(gens) or live:
        if started < len(gens):
            live.append(started)
            started += 1
        still = []
        for i in reversed(live):
            try:
                next(gens[i])
                still.append(i)
            except StopIteration as stop:
                results[i] = stop.value
        live = still[::-1]
    return results


def _gmlp_ffn_kernel(x_ref, g_ref, sc_ref, sh_ref, wi_ref, gv_ref, ws_ref, bs_ref, wo_ref, gt_ref,
                     g2_ref, sc2_ref, sh2_ref, w1_ref, w2_ref, gt2_ref, gf_ref, o_ref, u_ref, v_ref, *, final_norm):
    x = x_ref[...]
    h = _norm_mod(x, g_ref[...], sc_ref[...], sh_ref[...]).astype(BF16)
    half = wi_ref.shape[1] // 2
    gw = half // GM_GROUPS

    pw = 2 * gw
    n_pairs = GM_GROUPS // 2

    def project(c0, dst_ref, want_ssq):
        z = jnp.dot(h, wi_ref[:, c0:c0 + pw], preferred_element_type=F32)
        yield
        z = _gelu(z)
        d0 = c0 % half
        dst_ref[:, d0:d0 + pw] = z.astype(dst_ref.dtype)
        return jnp.sum(z * z, axis=-1, keepdims=True) if want_ssq else None

    cols = [(half + p * pw, v_ref, True) for p in range(n_pairs)] + [(p * pw, u_ref, False) for p in range(n_pairs)]
    ssq = functools.reduce(jnp.add, _skewed([project(*c) for c in cols])[:n_pairs])
    rinv = lax.rsqrt(ssq * (1.0 / half) + EPS)

    def mix(p):
        cs = slice(p * pw, (p + 1) * pw)
        vb = (v_ref[:, cs] * rinv * gv_ref[:, cs]).astype(BF16)
        mixed = [[jnp.dot(ws_ref[2 * p + k], vb[n * GM_CHUNK:(n + 1) * GM_CHUNK, k * gw:(k + 1) * gw],
                          preferred_element_type=F32) for k in range(2)] for n in range(TM // GM_CHUNK)]
        yield
        bias = [jnp.concatenate([bs_ref[2 * p + k]] * (gw // LANES), axis=1) for k in range(2)]
        t = jnp.concatenate([jnp.concatenate([row[k] + bias[k] for k in range(2)], axis=1) for row in mixed], axis=0)
        t = (t * u_ref[:, cs]).astype(BF16)
        return jnp.dot(t, wo_ref[cs, :], preferred_element_type=F32)

    y = functools.reduce(jnp.add, _skewed([mix(p) for p in range(n_pairs)]))
    x1 = x + gt_ref[...] * y
    o_ref[...] = _ffn_math(x1, g2_ref[...], sc2_ref[...], sh2_ref[...], w1_ref, w2_ref, gt2_ref[...], gf_ref[...],
                           final_norm)


def _gmlp_ffn(x, g, mods, l, w_in, g_v, ws, bs, w_out, o, g2, w1, w2, g_final, has_ctx, final_norm):
    b, t, d = x.shape
    half = w_in.shape[2] // 2
    assert (half // GM_GROUPS) % LANES == 0
    mod = lambda kind: _mod_spec(d, has_ctx, l, kind)
    return pl.pallas_call(
        functools.partial(_gmlp_ffn_kernel, final_norm=final_norm),
        grid=(b, t // TM),
        in_specs=[_tok_spec(d), _full_spec((1, d)), mod(SC1), mod(SH1),
                  _layer_spec(w_in, o), _full_spec((1, half)), _layer_spec(ws, o), _full_spec(bs.shape),
                  _layer_spec(w_out, o), mod(GT1),
                  _full_spec((1, d)), mod(SC2), mod(SH2), _layer_spec(w1, l), _layer_spec(w2, l), mod(GT2),
                  _full_spec((1, d))],
        out_specs=_tok_spec(d),
        out_shape=jax.ShapeDtypeStruct((b, t, d), F32),
        scratch_shapes=[pltpu.VMEM((TM, half), F32), pltpu.VMEM((TM, half), F32)],
        compiler_params=_cparams(2), name="gmlp_ffn",
    )(x, g, mods, mods, w_in, g_v, ws, bs, w_out, mods, g2, mods, mods, w1, w2, mods, g_final)


def kernel(x, c, ctx, c_ctx, w_ada, b_ada, g_norm_mix, g_norm_ffn, w_in_even, w_out_even, na_rpb, dn_conv, dn_a_log,
           dn_dt_bias, dn_g_out, w_in_odd, gm_g_v, gm_ws, gm_bs, w_out_odd, w_ff1, w_ff2, g_final):
    nb, seq, d = x.shape
    lc = ctx.shape[1]
    depth = w_ada.shape[0]
    assert nb <= CTX_ROW and seq % TM == 0 and lc == TM and seq % GRID_W == 0

    cond = jnp.zeros((MOD_ROWS, d), F32).at[:nb].set(c).at[CTX_ROW].set(c_ctx)
    mods = _adaln(cond, w_ada, b_ada)
    rope = _rope_tables(lc, seq)
    row = lambda a: a.reshape(1, -1).astype(F32)

    w_in_e = jnp.pad(w_in_even, ((0, 0), (0, 0), (0, LANES - 4 * DN_HEADS))).astype(BF16)
    w_out_e, w_in_o, w_out_o, ws_o = (w.astype(BF16) for w in (w_out_even, w_in_odd, w_out_odd, gm_ws))
    w1, w2 = w_ff1.astype(BF16), w_ff2.astype(BF16)

    xs = (ctx, x)
    has_ctx = True
    for l in range(depth):
        ctx_live = any(j % 2 == 0 for j in range(l + 1, depth))
        if l % 2 == 0:
            assert has_ctx
            e = l // 2
            qkv_a, qkv_b, gate, small = _even_in(xs, row(g_norm_mix[l]), mods, l, w_in_e, e)
            att = _na_attention(qkv_a, _na_bias_table(na_rpb[e]), lc, ctx_live)
            q, k, v, colsm, rowsm = _dn_prep(qkv_b, small, dn_conv[e], dn_a_log[e], dn_dt_bias[e], rope)
            o_f, o_b = _dn_chain(q, k, v, colsm, rowsm, lc)
            if isinstance(xs, tuple) and not ctx_live:
                xs = jnp.concatenate(xs, axis=1)
            xs = _even_out_ffn(xs, att, o_f, o_b, gate, row(dn_g_out[e]), w_out_e, e, row(g_norm_ffn[l]), mods,
                               w1, w2, l, row(g_final), ctx_live, l == depth - 1)
            has_ctx = ctx_live
        else:
            if isinstance(xs, tuple):
                xs = jnp.concatenate(xs, axis=1)
            o = l // 2
            bs = jnp.broadcast_to(gm_bs[o].astype(F32)[:, :, None], (GM_GROUPS, GM_CHUNK, LANES))
            xs = _gmlp_ffn(xs, row(g_norm_mix[l]), mods, l, w_in_o, row(gm_g_v[o]), ws_o, bs, w_out_o, o,
                           row(g_norm_ffn[l]), w1, w2, row(g_final), has_ctx, l == depth - 1)
    return xs[:, lc:] if has_ctx else xs
```
